```python
import jax, jax.numpy as jnp
from jax import lax
import numpy as np

D_MODEL = 1024
BATCH = 16
SEQ = 2048
DEPTH = 4

N_META = 16
N_MIXERS = 3
BLOCK = 128
PAD = BLOCK - N_META
EPS = 1e-6

RET_HEADS = 8
RET_DK = D_MODEL // RET_HEADS
RET_DV = 2 * D_MODEL // RET_HEADS
ROPE_BASE = 10000.0

SB_HEADS = 16
SB_HD = D_MODEL // SB_HEADS

POOL_WINDOWS = (2, 4, 8, 16)
POOL_GROUPS = 4
POOL_GD = D_MODEL // POOL_GROUPS

FFN_DIM = 2816
N_EXPERTS = 8
TOP_K = 2

N_RET = (DEPTH + 2) // 3
N_SB = (DEPTH + 1) // 3
N_POOL = DEPTH // 3
N_DENSE = (DEPTH + 1) // 2
N_MOE = DEPTH // 2

kernel_name = "hybrid_retention_stickbreak_pool_moe"


def _rmsnorm(x, g):
    xf = x.astype(jnp.float32)
    y = xf * lax.rsqrt(jnp.mean(jnp.square(xf), axis=-1, keepdims=True) + EPS) * g.astype(jnp.float32)
    return y.astype(x.dtype)


def _rotary(x, pos):
    half = x.shape[-1] // 2
    inv = ROPE_BASE ** (-jnp.arange(half, dtype=jnp.float32) / half)
    ang = pos.astype(jnp.float32)[:, None] * inv[None, :]
    cos = jnp.cos(ang)[None, :, None, :]
    sin = jnp.sin(ang)[None, :, None, :]
    x1, x2 = x[..., :half], x[..., half:]
    return jnp.concatenate([x1 * cos - x2 * sin, x1 * sin + x2 * cos], axis=-1)


def _retention(a, w_q, w_k, w_v, w_g, w_o, gn_g):
    f32 = jnp.float32
    B, L, _ = a.shape
    pos = jnp.arange(L)
    q = _rotary((a @ w_q).astype(f32).reshape(B, L, RET_HEADS, RET_DK), pos)
    k = _rotary((a @ w_k).astype(f32).reshape(B, L, RET_HEADS, RET_DK), pos) * (RET_DK ** -0.5)
    v = (a @ w_v).astype(f32).reshape(B, L, RET_HEADS, RET_DV)
    pad = ((0, 0), (PAD, 0), (0, 0), (0, 0))
    n = (L + PAD) // BLOCK
    qc = jnp.pad(q, pad).reshape(B, n, BLOCK, RET_HEADS, RET_DK)
    kc = jnp.pad(k, pad).reshape(B, n, BLOCK, RET_HEADS, RET_DK)
    vc = jnp.pad(v, pad).reshape(B, n, BLOCK, RET_HEADS, RET_DV)

    log_g = jnp.log1p(-jnp.exp2(-5.0 - jnp.arange(RET_HEADS, dtype=f32)))
    idx = jnp.arange(BLOCK, dtype=f32)
    diff = idx[:, None] - idx[None, :]
    decay = jnp.where(diff >= 0, jnp.exp(jnp.maximum(diff, 0.0)[None] * log_g[:, None, None]), 0.0)

    scores = jnp.einsum('bnchd,bnshd->bnhcs', qc, kc) * decay
    intra = jnp.einsum('bnhcs,bnshe->bnche', scores, vc)

    zeta = jnp.exp((BLOCK - 1 - idx)[:, None] * log_g[None, :])
    kv = jnp.einsum('bnshd,bnshe->bnhde', kc * zeta[None, None, :, :, None], vc)
    xi = jnp.exp((idx + 1)[:, None] * log_g[None, :])
    chunk_decay = jnp.exp(BLOCK * log_g)[None, :, None, None]

    def step(state, inp):
        q_i, kv_i = inp
        out = jnp.einsum('bchd,bhde->bche', q_i, state) * xi[None, :, :, None]
        return state * chunk_decay + kv_i, out

    state0 = jnp.zeros((B, RET_HEADS, RET_DK, RET_DV), f32)
    _, cross = lax.scan(step, state0, (jnp.moveaxis(qc, 1, 0), jnp.moveaxis(kv, 1, 0)))
    y = intra + jnp.moveaxis(cross, 0, 1)
    y = y.reshape(B, n * BLOCK, RET_HEADS, RET_DV)[:, PAD:]

    mu = jnp.mean(y, axis=-1, keepdims=True)
    var = jnp.mean(jnp.square(y - mu), axis=-1, keepdims=True)
    y = ((y - mu) * lax.rsqrt(var + EPS)).reshape(B, L, RET_HEADS * RET_DV) * gn_g.astype(f32)
    gate = jax.nn.silu((a @ w_g).astype(f32))
    return (gate * y).astype(a.dtype) @ w_o


def _stick_breaking(a, w_qkv, w_o):
    f32 = jnp.float32
    B, L, _ = a.shape
    qkv = (a @ w_qkv).astype(f32)
    q, k, v = jnp.split(qkv, 3, axis=-1)
    pad = ((0, 0), (PAD, 0), (0, 0), (0, 0))
    q = jnp.pad(q.reshape(B, L, SB_HEADS, SB_HD), pad)
    k = jnp.pad(k.reshape(B, L, SB_HEADS, SB_HD), pad)
    v = jnp.pad(v.reshape(B, L, SB_HEADS, SB_HD), pad)
    Lp = L + PAD
    n = Lp // BLOCK
    q_blocks = jnp.moveaxis(q.reshape(B, n, BLOCK, SB_HEADS, SB_HD), 1, 0)
    key_pos = jnp.arange(Lp)
    key_valid = key_pos >= PAD
    scale = SB_HD ** -0.5

    def block(args):
        qb, i = args
        qpos = i * BLOCK + jnp.arange(BLOCK)
        z = jnp.einsum('bchd,bshd->bhcs', qb, k) * scale
        mask = ((key_pos[None, :] < qpos[:, None]) & key_valid[None, :])[None, None]
        log_beta = jax.nn.log_sigmoid(z)
        log_1m = jnp.where(mask, jax.nn.log_sigmoid(-z), 0.0)
        suffix = lax.cumsum(log_1m, axis=3, reverse=True) - log_1m
        w = jnp.where(mask, jnp.exp(log_beta + suffix), 0.0)
        return jnp.einsum('bhcs,bshd->bchd', w, v)

    out = lax.map(block, (q_blocks, jnp.arange(n)))
    out = jnp.moveaxis(out, 0, 1).reshape(B, Lp, D_MODEL)[:, PAD:]
    return out.astype(a.dtype) @ w_o


def _pool_mixer(a, w_pool, scale):
    f32 = jnp.float32
    B, L, _ = a.shape
    af = a.astype(f32)
    cs = jnp.pad(jnp.cumsum(af, axis=1), ((0, 0), (1, 0), (0, 0)))
    t = jnp.arange(L)
    outs = []
    for g, w in enumerate(POOL_WINDOWS):
        sl = slice(g * POOL_GD, (g + 1) * POOL_GD)
        cs_g = cs[..., sl]
        lo = jnp.maximum(t + 1 - w, 0)
        cnt = (t + 1 - lo).astype(f32)
        mean = (cs_g[:, 1:] - cs_g[:, lo]) / cnt[None, :, None]
        outs.append((mean - af[..., sl]) @ w_pool[g].astype(f32))
    y = jnp.concatenate(outs, axis=-1) * scale.astype(f32)
    return y.astype(a.dtype)


def _swiglu(a, w1, w3, w2):
    return (jax.nn.silu(a @ w1) * (a @ w3)) @ w2


def _moe(a, w_router, w1, w3, w2):
    logits = (a @ w_router).astype(jnp.float32)
    top_v, top_i = lax.top_k(logits, TOP_K)
    gates = jax.nn.softmax(top_v, axis=-1)
    combine = jnp.sum(jax.nn.one_hot(top_i, N_EXPERTS, dtype=jnp.float32) * gates[..., None], axis=-2)
    out = jnp.zeros(a.shape, jnp.float32)
    for e in range(N_EXPERTS):
        out = out + combine[..., e:e + 1] * _swiglu(a, w1[e], w3[e], w2[e]).astype(jnp.float32)
    return out.astype(a.dtype)


def setup_inputs(seed: int = 0) -> dict:
    key = jax.random.key(seed)
    ks = jax.random.split(key, 24)
    f32 = jnp.float32

    def nrm(k, shape, fan_in):
        return jax.random.normal(k, shape, f32) * (fan_in ** -0.5)

    def gain(k, shape):
        return 1.0 + 0.02 * jax.random.normal(k, shape, f32)

    D = D_MODEL
    return {
        "x": jax.random.normal(ks[0], (BATCH, SEQ, D), f32),
        "meta_tokens": jax.random.normal(ks[1], (N_META, D), f32),
        "norm_mix_g": gain(ks[2], (DEPTH, D)),
        "norm_ffn_g": gain(ks[3], (DEPTH, D)),
        "ret_wq": nrm(ks[4], (N_RET, D, D), D),
        "ret_wk": nrm(ks[5], (N_RET, D, D), D),
        "ret_wv": nrm(ks[6], (N_RET, D, 2 * D), D),
        "ret_wg": nrm(ks[7], (N_RET, D, 2 * D), D),
        "ret_wo": nrm(ks[8], (N_RET, 2 * D, D), 2 * D),
        "ret_gn_g": gain(ks[9], (N_RET, 2 * D)),
        "sb_wqkv": nrm(ks[10], (N_SB, D, 3 * D), D),
        "sb_wo": nrm(ks[11], (N_SB, D, D), D),
        "pool_w": nrm(ks[12], (N_POOL, POOL_GROUPS, POOL_GD, POOL_GD), POOL_GD),
        "pool_scale": 1.0 + 0.1 * jax.random.normal(ks[13], (N_POOL, D), f32),
        "ffn_w1": nrm(ks[14], (N_DENSE, D, FFN_DIM), D),
        "ffn_w3": nrm(ks[15], (N_DENSE, D, FFN_DIM), D),
        "ffn_w2": nrm(ks[16], (N_DENSE, FFN_DIM, D), FFN_DIM),
        "moe_router": nrm(ks[17], (N_MOE, D, N_EXPERTS), D),
        "moe_w1": nrm(ks[18], (N_MOE, N_EXPERTS, D, FFN_DIM), D),
        "moe_w3": nrm(ks[19], (N_MOE, N_EXPERTS, D, FFN_DIM), D),
        "moe_w2": nrm(ks[20], (N_MOE, N_EXPERTS, FFN_DIM, D), FFN_DIM),
        "final_norm_g": gain(ks[21], (D,)),
    }


def reference(x, meta_tokens, norm_mix_g, norm_ffn_g, ret_wq, ret_wk, ret_wv, ret_wg, ret_wo, ret_gn_g,
              sb_wqkv, sb_wo, pool_w, pool_scale, ffn_w1, ffn_w3, ffn_w2,
              moe_router, moe_w1, moe_w3, moe_w2, final_norm_g):
    B = x.shape[0]
    meta = jnp.broadcast_to(meta_tokens.astype(x.dtype)[None], (B, N_META, D_MODEL))
    h = jnp.concatenate([meta, x], axis=1)
    for i in range(DEPTH):
        m = i % N_MIXERS
        j = i // N_MIXERS
        a = _rmsnorm(h, norm_mix_g[i])
        if m == 0:
            y = _retention(a, ret_wq[j], ret_wk[j], ret_wv[j], ret_wg[j], ret_wo[j], ret_gn_g[j])
        elif m == 1:
            y = _stick_breaking(a, sb_wqkv[j], sb_wo[j])
        else:
            y = _pool_mixer(a, pool_w[j], pool_scale[j])
        h = h + y
        a = _rmsnorm(h, norm_ffn_g[i])
        c = i // 2
        if i % 2 == 0:
            y = _swiglu(a, ffn_w1[c], ffn_w3[c], ffn_w2[c])
        else:
            y = _moe(a, moe_router[c], moe_w1[c], moe_w3[c], moe_w2[c])
        h = h + y
    h = _rmsnorm(h, final_norm_g)
    return h[:, N_META:]
```

```python
import functools

import numpy as np
import jax
import jax.numpy as jnp
from jax import lax
from jax.experimental import pallas as pl
from jax.experimental.pallas import tpu as pltpu

F32 = jnp.float32
BF16 = jnp.bfloat16

D_MODEL = 1024
N_META = 16
BLOCK = 128
PAD = BLOCK - N_META
EPS = 1e-6
RET_HEADS = 8
RET_DK = D_MODEL // RET_HEADS
RET_DV = 2 * D_MODEL // RET_HEADS
ROPE_BASE = 10000.0
SB_HEADS = 16
SB_HD = D_MODEL // SB_HEADS
POOL_WINDOWS = (2, 4, 8, 16)
POOL_GD = D_MODEL // len(POOL_WINDOWS)
N_EXPERTS = 8
LANES = 128
VMEM_LIMIT_BYTES = 58 * 1024 * 1024


def _cparams(*sem):
    return pltpu.CompilerParams(dimension_semantics=sem, vmem_limit_bytes=VMEM_LIMIT_BYTES)


def _pick(n, cands):
    for c in cands:
        if n % c == 0:
            return c
    raise ValueError(f"no tile for {n} in {cands}")


def _rms(x, g):
    ms = jnp.mean(x * x, axis=-1, keepdims=True)
    return x * lax.rsqrt(ms + EPS) * g


def _silu(x):
    return x * (1.0 / (1.0 + jnp.exp(-x)))


def _norm_matmul_kernel(h_ref, g_ref, w_ref, o_ref, a_scr):
    @pl.when(pl.program_id(1) == 0)
    def _():
        a_scr[...] = _rms(h_ref[...], g_ref[...]).astype(BF16)

    o_ref[...] = jnp.dot(a_scr[...], w_ref[...], preferred_element_type=F32).astype(o_ref.dtype)


def _norm_matmul(h2, g, w):
    R, D = h2.shape
    N = w.shape[1]
    tm = _pick(R, (1024, 512, 256, 128))
    tn = _pick(N, (1024, 512, 256, 128))
    return pl.pallas_call(
        _norm_matmul_kernel,
        grid=(R // tm, N // tn),
        in_specs=[pl.BlockSpec((tm, D), lambda i, j: (i, 0)),
                  pl.BlockSpec((1, D), lambda i, j: (0, 0)),
                  pl.BlockSpec((D, tn), lambda i, j: (0, j))],
        out_specs=pl.BlockSpec((tm, tn), lambda i, j: (i, j)),
        out_shape=jax.ShapeDtypeStruct((R, N), BF16),
        scratch_shapes=[pltpu.VMEM((tm, D), BF16)],
        compiler_params=_cparams("parallel", "arbitrary"),
        name="norm_matmul",
    )(h2, g.reshape(1, D), w)


def _matmul_residual_kernel(y_ref, w_ref, h_ref, o_ref):
    o_ref[...] = h_ref[...] + jnp.dot(y_ref[...], w_ref[...], preferred_element_type=F32)


def _matmul_residual(y, w, h2):
    R, K = y.shape
    D = w.shape[1]
    tm = _pick(R, (512, 256, 128))
    return pl.pallas_call(
        _matmul_residual_kernel,
        grid=(R // tm,),
        in_specs=[pl.BlockSpec((tm, K), lambda i: (i, 0)),
                  pl.BlockSpec((K, D), lambda i: (0, 0)),
                  pl.BlockSpec((tm, D), lambda i: (i, 0))],
        out_specs=pl.BlockSpec((tm, D), lambda i: (i, 0)),
        out_shape=jax.ShapeDtypeStruct((R, D), F32),
        input_output_aliases={2: 0},
        compiler_params=_cparams("parallel"),
        name="matmul_residual",
    )(y, w, h2)


def _retention_kernel(q_ref, k_ref, v_ref, g_ref, cos_ref, sin_ref, dec_ref, zeta_ref, xi_ref, cd_ref,
                      gn_ref, o_ref, state_scr):
    n_chunks = q_ref.shape[0] // BLOCK
    state_scr[...] = jnp.zeros_like(state_scr)

    def chunk(c, carry):
        rows = pl.ds(pl.multiple_of(c * BLOCK, BLOCK), BLOCK)
        cos = cos_ref[rows, :]
        sin = sin_ref[rows, :]
        q = q_ref[rows, :].astype(F32)
        k = k_ref[rows, :].astype(F32)
        qr = q * cos + pltpu.roll(q, RET_DK // 2, 1) * sin
        kr = (k * cos + pltpu.roll(k, RET_DK // 2, 1) * sin) * (RET_DK ** -0.5)
        qb = qr.astype(BF16)
        kb = kr.astype(BF16)
        v = v_ref[rows, :]
        s = lax.dot_general(qb, kb, (((1,), (1,)), ((), ())), preferred_element_type=F32) * dec_ref[...]
        intra = jnp.dot(s.astype(BF16), v, preferred_element_type=F32)
        st = state_scr[...]
        cross = jnp.dot(qb, st.astype(BF16), preferred_element_type=F32) * xi_ref[...]
        y = intra + cross
        kz = (kr * zeta_ref[...]).astype(BF16)
        kv = lax.dot_general(kz, v, (((0,), (0,)), ((), ())), preferred_element_type=F32)
        state_scr[...] = st * cd_ref[...] + kv
        mu = jnp.mean(y, axis=-1, keepdims=True)
        d = y - mu
        var = jnp.mean(d * d, axis=-1, keepdims=True)
        yn = d * lax.rsqrt(var + EPS) * gn_ref[...]
        o_ref[rows, :] = (_silu(g_ref[rows, :].astype(F32)) * yn).astype(o_ref.dtype)
        return carry

    lax.fori_loop(0, n_chunks, chunk, 0)


def _retention_tables(Lp):
    half = RET_DK // 2
    inv = ROPE_BASE ** (-jnp.arange(half, dtype=F32) / half)
    pos = (jnp.arange(Lp) - PAD).astype(F32)
    ang = pos[:, None] * inv[None, :]
    cos, sin = jnp.cos(ang), jnp.sin(ang)
    cos2 = jnp.concatenate([cos, cos], axis=-1)
    sin2 = jnp.concatenate([-sin, sin], axis=-1)
    log_g = jnp.log1p(-jnp.exp2(-5.0 - jnp.arange(RET_HEADS, dtype=F32)))
    idx = jnp.arange(BLOCK, dtype=F32)
    diff = idx[:, None] - idx[None, :]
    decay = jnp.where(diff >= 0, jnp.exp(jnp.maximum(diff, 0.0)[None] * log_g[:, None, None]), 0.0)
    zeta = jnp.exp((BLOCK - 1 - idx)[None, :] * log_g[:, None])
    xi = jnp.exp((idx + 1)[None, :] * log_g[:, None])
    cd = jnp.exp(BLOCK * log_g)
    zeta_b = jnp.broadcast_to(zeta[:, :, None], (RET_HEADS, BLOCK, RET_DK))
    xi_b = jnp.broadcast_to(xi[:, :, None], (RET_HEADS, BLOCK, RET_DV))
    cd_b = jnp.broadcast_to(cd[:, None, None], (RET_HEADS, 1, RET_DV))
    return cos2, sin2, decay, zeta_b, xi_b, cd_b


def _retention(p3, gn_g):
    B, Lp, _ = p3.shape
    cos2, sin2, decay, zeta_b, xi_b, cd_b = _retention_tables(Lp)
    nq = D_MODEL // RET_DK
    nv = 2 * D_MODEL // RET_DV
    return pl.pallas_call(
        _retention_kernel,
        grid=(B, RET_HEADS),
        in_specs=[pl.BlockSpec((None, Lp, RET_DK), lambda b, h: (b, 0, h)),
                  pl.BlockSpec((None, Lp, RET_DK), lambda b, h: (b, 0, nq + h)),
                  pl.BlockSpec((None, Lp, RET_DV), lambda b, h: (b, 0, nv + h)),
                  pl.BlockSpec((None, Lp, RET_DV), lambda b, h: (b, 0, 2 * nv + h)),
                  pl.BlockSpec((Lp, RET_DK), lambda b, h: (0, 0)),
                  pl.BlockSpec((Lp, RET_DK), lambda b, h: (0, 0)),
                  pl.BlockSpec((None, BLOCK, BLOCK), lambda b, h: (h, 0, 0)),
                  pl.BlockSpec((None, BLOCK, RET_DK), lambda b, h: (h, 0, 0)),
                  pl.BlockSpec((None, BLOCK, RET_DV), lambda b, h: (h, 0, 0)),
                  pl.BlockSpec((None, 1, RET_DV), lambda b, h: (h, 0, 0)),
                  pl.BlockSpec((1, RET_DV), lambda b, h: (0, h))],
        out_specs=pl.BlockSpec((None, Lp, RET_DV), lambda b, h: (b, 0, h)),
        out_shape=jax.ShapeDtypeStruct((B, Lp, 2 * D_MODEL), BF16),
        scratch_shapes=[pltpu.VMEM((RET_DK, RET_DV), F32)],
        compiler_params=_cparams("parallel", "arbitrary"),
        name="retention",
    )(p3, p3, p3, p3, cos2, sin2, decay, zeta_b, xi_b, cd_b, gn_g.reshape(1, 2 * D_MODEL))


def _sb_kernel(q_ref, k_ref, v_ref, u_ref, o_ref, acc_scr, car_scr):
    i = pl.program_id(2)
    lane = lax.broadcasted_iota(jnp.int32, (BLOCK, LANES), 1)
    qs = q_ref[...] * (SB_HD ** -0.5)
    zero = jnp.zeros_like(qs)
    qh = (jnp.where(lane < SB_HD, qs, zero), jnp.where(lane >= SB_HD, qs, zero))
    row = lax.broadcasted_iota(jnp.int32, (BLOCK, BLOCK), 0)
    col = lax.broadcasted_iota(jnp.int32, (BLOCK, BLOCK), 1)
    qpos = i * BLOCK + row
    acc_scr[...] = jnp.zeros_like(acc_scr)
    car_scr[...] = jnp.zeros_like(car_scr)

    def key_block(t, carry):
        j = i - t
        rows = pl.ds(pl.multiple_of(j * BLOCK, BLOCK), BLOCK)
        kj = k_ref[rows, :]
        vj = v_ref[rows, :]
        kpos = j * BLOCK + col
        mask = (kpos < qpos) & (kpos >= PAD)
        for h in range(2):
            z = lax.dot_general(qh[h], kj, (((1,), (1,)), ((), ())), preferred_element_type=F32)
            l = jnp.log(1.0 + jnp.exp(-jnp.abs(z)))
            log_beta = jnp.minimum(z, 0.0) - l
            log_1m = jnp.where(mask, log_beta - z, 0.0)
            r = jnp.dot(log_1m.astype(BF16), u_ref[...], preferred_element_type=F32)
            suffix = car_scr[h] + r[:, :BLOCK]
            w = jnp.where(mask, jnp.exp(log_beta + suffix), 0.0)
            acc_scr[h] += jnp.dot(w.astype(BF16), vj, preferred_element_type=F32)
            car_scr[h] += r[:, BLOCK:]
        return carry

    lax.fori_loop(0, i + 1, key_block, 0)
    o_ref[...] = jnp.where(lane < SB_HD, acc_scr[0], acc_scr[1]).astype(o_ref.dtype)


def _stick_breaking(qkv3):
    B, Lp, _ = qkv3.shape
    n = Lp // BLOCK
    npair = D_MODEL // LANES
    j_idx = np.arange(BLOCK)
    u = np.concatenate([(j_idx[:, None] > j_idx[None, :]).astype(np.float32),
                        np.ones((BLOCK, BLOCK), np.float32)], axis=1)
    return pl.pallas_call(
        _sb_kernel,
        grid=(B, npair, n),
        in_specs=[pl.BlockSpec((None, BLOCK, LANES), lambda b, p, i: (b, i, p)),
                  pl.BlockSpec((None, Lp, LANES), lambda b, p, i: (b, 0, npair + p)),
                  pl.BlockSpec((None, Lp, LANES), lambda b, p, i: (b, 0, 2 * npair + p)),
                  pl.BlockSpec((BLOCK, 2 * BLOCK), lambda b, p, i: (0, 0))],
        out_specs=pl.BlockSpec((None, BLOCK, LANES), lambda b, p, i: (b, i, p)),
        out_shape=jax.ShapeDtypeStruct((B, Lp, D_MODEL), BF16),
        scratch_shapes=[pltpu.VMEM((2, BLOCK, LANES), F32), pltpu.VMEM((2, BLOCK, BLOCK), F32)],
        compiler_params=_cparams("parallel", "parallel", "arbitrary"),
        name="stick_breaking",
    )(qkv3, qkv3, qkv3, jnp.asarray(u, BF16))


def _pool_kernel(h_ref, g_ref, bm_ref, bh_ref, w_ref, sc_ref, o_ref, prev_scr):
    j = pl.program_id(1)

    @pl.when(j == 0)
    def _():
        prev_scr[...] = jnp.zeros_like(prev_scr)

    x = h_ref[...]
    pos = j * BLOCK + lax.broadcasted_iota(jnp.int32, (BLOCK, POOL_GD), 0) - PAD
    a = _rms(x, g_ref[...])
    a_hi = a.astype(BF16)
    a_lo = (a - a_hi.astype(F32)).astype(BF16)
    for g, win in enumerate(POOL_WINDOWS):
        sl = slice(g * POOL_GD, (g + 1) * POOL_GD)
        ws = (jnp.dot(bm_ref[g], a_hi[:, sl], preferred_element_type=F32)
              + jnp.dot(bm_ref[g], a_lo[:, sl], preferred_element_type=F32)
              + jnp.dot(bh_ref[g], prev_scr[0, :, sl], preferred_element_type=F32)
              + jnp.dot(bh_ref[g], prev_scr[1, :, sl], preferred_element_type=F32))
        cnt = jnp.clip(pos + 1, 1, win).astype(F32)
        diff = ws / cnt - a[:, sl]
        y = jnp.dot(diff.astype(BF16), w_ref[g], preferred_element_type=F32) * sc_ref[:, sl]
        o_ref[:, sl] = x[:, sl] + y
    prev_scr[0] = a_hi
    prev_scr[1] = a_lo


def _pool_mixer(h3, g, w_pool, scale):
    B, Lp, D = h3.shape
    t = np.arange(BLOCK)
    d_main = t[:, None] - t[None, :]
    d_halo = t[:, None] + BLOCK - t[None, :]
    bm = np.stack([((d_main >= 0) & (d_main < w)) for w in POOL_WINDOWS]).astype(np.float32)
    bh = np.stack([(d_halo < w) for w in POOL_WINDOWS]).astype(np.float32)
    ng = len(POOL_WINDOWS)
    return pl.pallas_call(
        _pool_kernel,
        grid=(B, Lp // BLOCK),
        in_specs=[pl.BlockSpec((None, BLOCK, D), lambda b, j: (b, j, 0)),
                  pl.BlockSpec((1, D), lambda b, j: (0, 0)),
                  pl.BlockSpec((ng, BLOCK, BLOCK), lambda b, j: (0, 0, 0)),
                  pl.BlockSpec((ng, BLOCK, BLOCK), lambda b, j: (0, 0, 0)),
                  pl.BlockSpec((ng, POOL_GD, POOL_GD), lambda b, j: (0, 0, 0)),
                  pl.BlockSpec((1, D), lambda b, j: (0, 0))],
        out_specs=pl.BlockSpec((None, BLOCK, D), lambda b, j: (b, j, 0)),
        out_shape=jax.ShapeDtypeStruct((B, Lp, D), F32),
        scratch_shapes=[pltpu.VMEM((2, BLOCK, D), BF16)],
        input_output_aliases={0: 0},
        compiler_params=_cparams("parallel", "arbitrary"),
        name="pool_mixer",
    )(h3, g.reshape(1, D), jnp.asarray(bm, BF16), jnp.asarray(bh, BF16), w_pool.astype(BF16),
      scale.reshape(1, D))


def _ffn_kernel(te_ref, nu_ref, x_ref, g_ref, w1_ref, w3_ref, w2_ref, o_ref, a_scr, *, fused, fc):
    t = pl.program_id(0)

    @pl.when(t < nu_ref[0])
    def _():
        x = x_ref[...]
        a_scr[...] = (_rms(x, g_ref[...]) if fused else x).astype(BF16)
        for c in range(w1_ref.shape[-1] // fc):
            cs = slice(c * fc, (c + 1) * fc)
            a = a_scr[...]
            h1 = jnp.dot(a, w1_ref[0, :, cs], preferred_element_type=F32)
            h3 = jnp.dot(a, w3_ref[0, :, cs], preferred_element_type=F32)
            y = jnp.dot((_silu(h1) * h3).astype(BF16), w2_ref[0, cs, :], preferred_element_type=F32)
            if c == 0:
                o_ref[...] = x + y if fused else y
            else:
                o_ref[...] += y

    @pl.when(t >= nu_ref[0])
    def _():
        o_ref[...] = jnp.zeros_like(o_ref)


def _ffn(x2, g, w1, w3, w2, tile_expert, n_used, *, fused):
    R, D = x2.shape
    F = w1.shape[-1]
    tm = _pick(R, (512, 256, 128))
    kern = functools.partial(_ffn_kernel, fused=fused, fc=256)
    grid_spec = pltpu.PrefetchScalarGridSpec(
        num_scalar_prefetch=2,
        grid=(R // tm,),
        in_specs=[pl.BlockSpec((tm, D), lambda t, te, nu: (t, 0)),
                  pl.BlockSpec((1, D), lambda t, te, nu: (0, 0)),
                  pl.BlockSpec((1, D, F), lambda t, te, nu: (te[t], 0, 0)),
                  pl.BlockSpec((1, D, F), lambda t, te, nu: (te[t], 0, 0)),
                  pl.BlockSpec((1, F, D), lambda t, te, nu: (te[t], 0, 0))],
        out_specs=pl.BlockSpec((tm, D), lambda t, te, nu: (t, 0)),
        scratch_shapes=[pltpu.VMEM((tm, D), BF16)])
    return pl.pallas_call(
        kern,
        grid_spec=grid_spec,
        out_shape=jax.ShapeDtypeStruct((R, D), F32),
        input_output_aliases={2: 0} if fused else {},
        compiler_params=_cparams("arbitrary"),
        name="swiglu_fused" if fused else "swiglu_experts",
    )(tile_expert, n_used, x2, g.reshape(1, D), w1, w3, w2)


def _router_kernel(h_ref, g_ref, wr_ref, tri_ref, slab_ref, cnt_ref, carry_scr):
    first = (pl.program_id(0) == 0) & (pl.program_id(1) == 0)

    @pl.when(first)
    def _():
        carry_scr[...] = jnp.zeros_like(carry_scr)

    tr = h_ref.shape[0]
    a = _rms(h_ref[...], g_ref[...])
    a_hi = a.astype(BF16)
    a_lo = (a - a_hi.astype(F32)).astype(BF16)
    wr = wr_ref[...]
    w_hi = wr.astype(BF16)
    w_lo = (wr - w_hi.astype(F32)).astype(BF16)
    logits = (jnp.dot(a_hi, w_hi, preferred_element_type=F32)
              + jnp.dot(a_hi, w_lo, preferred_element_type=F32)
              + jnp.dot(a_lo, w_hi, preferred_element_type=F32))
    lane = lax.broadcasted_iota(jnp.int32, (tr, LANES), 1).astype(F32)
    neg = jnp.float32(-jnp.inf)
    logits = jnp.where(lane < N_EXPERTS, logits, neg)
    m1 = jnp.max(logits, axis=-1, keepdims=True)
    i1 = jnp.min(jnp.where(logits == m1, lane, float(LANES)), axis=-1, keepdims=True)
    rest = jnp.where(lane == i1, neg, logits)
    m2 = jnp.max(rest, axis=-1, keepdims=True)
    i2 = jnp.min(jnp.where(rest == m2, lane, float(LANES)), axis=-1, keepdims=True)
    e = jnp.exp(m2 - m1)
    g1 = 1.0 / (1.0 + e)
    g2 = e / (1.0 + e)
    pos = pl.program_id(1) * tr + lax.broadcasted_iota(jnp.int32, (tr, LANES), 0)
    valid = (pos >= PAD).astype(F32)
    oh1 = (lane == i1).astype(F32)
    oh2 = (lane == i2).astype(F32)
    chosen = (oh1 + oh2) * valid
    before = carry_scr[0:1, :] + jnp.dot(tri_ref[...], chosen.astype(BF16), preferred_element_type=F32)
    r1 = jnp.sum(before * oh1, axis=-1, keepdims=True)
    r2 = jnp.sum(before * oh2, axis=-1, keepdims=True)
    total = carry_scr[0:1, :] + jnp.sum(chosen, axis=0, keepdims=True)
    carry_scr[...] = jnp.broadcast_to(total, carry_scr.shape)
    cnt_ref[...] = jnp.broadcast_to(total, cnt_ref.shape)
    cols = (i1, i2, g1 * valid, g2 * valid, r1, r2)
    slab = jnp.zeros((tr, LANES), F32)
    for c, val in enumerate(cols):
        slab = jnp.where(lane == c, val, slab)
    slab_ref[...] = slab


def _router(h3, g, w_router):
    B, Lp, D = h3.shape
    tr = Lp // 2
    wr = jnp.zeros((D, LANES), F32).at[:, :N_EXPERTS].set(w_router)
    t = np.arange(tr)
    tri = (t[:, None] > t[None, :]).astype(np.float32)
    slab, cnt = pl.pallas_call(
        _router_kernel,
        grid=(B, Lp // tr),
        in_specs=[pl.BlockSpec((None, tr, D), lambda b, j: (b, j, 0)),
                  pl.BlockSpec((1, D), lambda b, j: (0, 0)),
                  pl.BlockSpec((D, LANES), lambda b, j: (0, 0)),
                  pl.BlockSpec((tr, tr), lambda b, j: (0, 0))],
        out_specs=[pl.BlockSpec((tr, LANES), lambda b, j: (b * (Lp // tr) + j, 0)),
                   pl.BlockSpec((8, LANES), lambda b, j: (0, 0))],
        out_shape=[jax.ShapeDtypeStruct((B * Lp, LANES), F32), jax.ShapeDtypeStruct((8, LANES), F32)],
        scratch_shapes=[pltpu.VMEM((8, LANES), F32)],
        compiler_params=_cparams("arbitrary", "arbitrary"),
        name="moe_router",
    )(h3, g.reshape(1, D), wr, jnp.asarray(tri, BF16))
    return slab, cnt[0, :N_EXPERTS]


def _dispatch_kernel(p1_ref, p2_ref, h_ref, g_ref, xs_in_ref, xs_ref, a_scr, sem):
    del xs_in_ref
    b = pl.program_id(0)
    j = pl.program_id(1)
    nb = pl.num_programs(0)
    td = h_ref.shape[0]
    first_row = (PAD, 0)

    def wait_slot(s):
        n = td - first_row[s]
        for _ in range(2):
            pltpu.make_async_copy(a_scr.at[s, pl.ds(0, n)], xs_ref.at[pl.ds(0, n)], sem.at[s]).wait()

    for s in range(2):
        @pl.when((j == s) & (b > 0))
        def _(s=s):
            wait_slot(s)

    a = _rms(h_ref[...], g_ref[...])
    for s in range(2):
        @pl.when(j == s)
        def _(s=s):
            a_scr[s] = a
            base = (b * 2 + s) * td

            def row(r, carry):
                src = a_scr.at[s, pl.ds(r, 1)]
                pltpu.make_async_copy(src, xs_ref.at[pl.ds(p1_ref[base + r], 1)], sem.at[s]).start()
                pltpu.make_async_copy(src, xs_ref.at[pl.ds(p2_ref[base + r], 1)], sem.at[s]).start()
                return carry

            lax.fori_loop(first_row[s], td, row, 0)

    @pl.when((b == nb - 1) & (j == 1))
    def _():
        wait_slot(0)
        wait_slot(1)


def _dispatch(h3, g, p1, p2, n_rows):
    B, Lp, D = h3.shape
    td = Lp // 2
    grid_spec = pltpu.PrefetchScalarGridSpec(
        num_scalar_prefetch=2,
        grid=(B, 2),
        in_specs=[pl.BlockSpec((None, td, D), lambda b, j, p1, p2: (b, j, 0)),
                  pl.BlockSpec((1, D), lambda b, j, p1, p2: (0, 0)),
                  pl.BlockSpec(memory_space=pl.ANY)],
        out_specs=pl.BlockSpec(memory_space=pl.ANY),
        scratch_shapes=[pltpu.VMEM((2, td, D), F32), pltpu.SemaphoreType.DMA((2,))])
    return pl.pallas_call(
        _dispatch_kernel,
        grid_spec=grid_spec,
        out_shape=jax.ShapeDtypeStruct((n_rows, D), F32),
        input_output_aliases={4: 0},
        compiler_params=_cparams("arbitrary", "arbitrary"),
        name="moe_dispatch",
    )(p1, p2, h3, g.reshape(1, D), jnp.zeros((n_rows, D), F32))


def _combine_kernel(p1_ref, p2_ref, h_ref, slab_ref, y_ref, o_ref, buf, sem):
    i = pl.program_id(0)
    n = pl.num_programs(0)
    tc = h_ref.shape[0]

    def start(step, slot):
        base = step * tc

        def row(r, carry):
            pltpu.make_async_copy(y_ref.at[pl.ds(p1_ref[base + r], 1)], buf.at[slot, 0, pl.ds(r, 1)],
                                  sem.at[slot]).start()
            pltpu.make_async_copy(y_ref.at[pl.ds(p2_ref[base + r], 1)], buf.at[slot, 1, pl.ds(r, 1)],
                                  sem.at[slot]).start()
            return carry

        lax.fori_loop(0, tc, row, 0)

    @pl.when(i == 0)
    def _():
        start(0, 0)

    for s in range(2):
        @pl.when((i + 1 < n) & ((i + 1) % 2 == s))
        def _(s=s):
            start(i + 1, s)

    for s in range(2):
        @pl.when(i % 2 == s)
        def _(s=s):
            for c in range(2):
                pltpu.make_async_copy(y_ref.at[pl.ds(0, tc)], buf.at[s, c], sem.at[s]).wait()
            slab = slab_ref[...]
            o_ref[...] = h_ref[...] + (slab[:, 2:3] * buf[s, 0] + slab[:, 3:4] * buf[s, 1])


def _combine(h2, slab, y, p1, p2):
    R, D = h2.shape
    tc = _pick(R, (256, 128))
    grid_spec = pltpu.PrefetchScalarGridSpec(
        num_scalar_prefetch=2,
        grid=(R // tc,),
        in_specs=[pl.BlockSpec((tc, D), lambda i, p1, p2: (i, 0)),
                  pl.BlockSpec((tc, LANES), lambda i, p1, p2: (i, 0)),
                  pl.BlockSpec(memory_space=pl.ANY)],
        out_specs=pl.BlockSpec((tc, D), lambda i, p1, p2: (i, 0)),
        scratch_shapes=[pltpu.VMEM((2, 2, tc, D), F32), pltpu.SemaphoreType.DMA((2,))])
    return pl.pallas_call(
        _combine_kernel,
        grid_spec=grid_spec,
        out_shape=jax.ShapeDtypeStruct((R, D), F32),
        input_output_aliases={2: 0},
        compiler_params=_cparams("arbitrary"),
        name="moe_combine",
    )(p1, p2, h2, slab, y)


def _moe(h3, g, w_router, w1, w3, w2):
    B, Lp, D = h3.shape
    tm = 512
    n_tokens = B * (Lp - PAD)
    n_tiles = (2 * n_tokens) // tm + N_EXPERTS
    slab, counts = _router(h3, g, w_router)
    counts = counts.astype(jnp.int32)
    tiles_e = (counts + tm - 1) // tm
    end_tile = jnp.cumsum(tiles_e)
    offset = ((end_tile - tiles_e) * tm).astype(F32)
    n_used = end_tile[-1:]
    expert_ids = jnp.arange(N_EXPERTS, dtype=F32)
    e1, e2 = slab[:, 0:1], slab[:, 1:2]
    valid = (jnp.arange(B * Lp) % Lp) >= PAD
    off1 = jnp.sum(jnp.where(e1 == expert_ids[None, :], offset[None, :], 0.0), axis=-1)
    off2 = jnp.sum(jnp.where(e2 == expert_ids[None, :], offset[None, :], 0.0), axis=-1)
    p1 = jnp.where(valid, off1 + slab[:, 4], 0.0).astype(jnp.int32)
    p2 = jnp.where(valid, off2 + slab[:, 5], 0.0).astype(jnp.int32)
    tile_ids = jnp.arange(n_tiles, dtype=jnp.int32)
    last_tile = jnp.maximum(n_used - 1, 0)
    tile_expert = jnp.sum(jnp.minimum(tile_ids, last_tile)[:, None] >= end_tile[None, :], axis=-1).astype(jnp.int32)
    tile_expert = jnp.minimum(tile_expert, N_EXPERTS - 1)
    xs = _dispatch(h3, g, p1, p2, n_tiles * tm)
    ys = _ffn(xs, g, w1, w3, w2, tile_expert, n_used.astype(jnp.int32), fused=False)
    return _combine(h3.reshape(B * Lp, D), slab, ys, p1, p2).reshape(B, Lp, D)


def _final_norm_kernel(h_ref, g_ref, o_ref):
    o_ref[...] = _rms(h_ref[...], g_ref[...])


def _final_norm(h3, g):
    B, Lp, D = h3.shape
    lead = (PAD + N_META) // BLOCK
    n = Lp // BLOCK - lead
    return pl.pallas_call(
        _final_norm_kernel,
        grid=(B, n),
        in_specs=[pl.BlockSpec((None, BLOCK, D), lambda b, j: (b, j + lead, 0)),
                  pl.BlockSpec((1, D), lambda b, j: (0, 0))],
        out_specs=pl.BlockSpec((None, BLOCK, D), lambda b, j: (b, j, 0)),
        out_shape=jax.ShapeDtypeStruct((B, n * BLOCK, D), F32),
        compiler_params=_cparams("parallel", "parallel"),
        name="final_norm",
    )(h3, g.reshape(1, D))


def kernel(x, meta_tokens, norm_mix_g, norm_ffn_g, ret_wq, ret_wk, ret_wv, ret_wg, ret_wo, ret_gn_g,
           sb_wqkv, sb_wo, pool_w, pool_scale, ffn_w1, ffn_w3, ffn_w2,
           moe_router, moe_w1, moe_w3, moe_w2, final_norm_g):
    B, S, D = x.shape
    Lp = PAD + N_META + S
    R = B * Lp
    depth = norm_mix_g.shape[0]
    meta = jnp.broadcast_to(meta_tokens.astype(x.dtype)[None], (B, N_META, D))
    h = jnp.concatenate([jnp.zeros((B, PAD, D), x.dtype), meta, x], axis=1)
    dense_tiles = jnp.zeros((R // _pick(R, (512, 256, 128)),), jnp.int32)
    all_tiles = jnp.full((1,), dense_tiles.shape[0], jnp.int32)
    for i in range(depth):
        m, j = i % 3, i // 3
        if m == 0:
            w_in = jnp.concatenate([ret_wq[j], ret_wk[j], ret_wv[j], ret_wg[j]], axis=1).astype(BF16)
            p = _norm_matmul(h.reshape(R, D), norm_mix_g[i], w_in)
            y = _retention(p.reshape(B, Lp, -1), ret_gn_g[j])
            h = _matmul_residual(y.reshape(R, -1), ret_wo[j].astype(BF16), h.reshape(R, D)).reshape(B, Lp, D)
        elif m == 1:
            qkv = _norm_matmul(h.reshape(R, D), norm_mix_g[i], sb_wqkv[j].astype(BF16))
            y = _stick_breaking(qkv.reshape(B, Lp, -1))
            h = _matmul_residual(y.reshape(R, D), sb_wo[j].astype(BF16), h.reshape(R, D)).reshape(B, Lp, D)
        else:
            h = _pool_mixer(h, norm_mix_g[i], pool_w[j], pool_scale[j])
        c = i // 2
        if i % 2 == 0:
            h = _ffn(h.reshape(R, D), norm_ffn_g[i], ffn_w1[c:c + 1].astype(BF16), ffn_w3[c:c + 1].astype(BF16),
                     ffn_w2[c:c + 1].astype(BF16), dense_tiles, all_tiles, fused=True).reshape(B, Lp, D)
        else:
            h = _moe(h, norm_ffn_g[i], moe_router[c], moe_w1[c].astype(BF16), moe_w3[c].astype(BF16),
                     moe_w2[c].astype(BF16))
    return _final_norm(h, final_norm_g)
```

```python
import functools

import numpy as np
import jax
import jax.numpy as jnp
from jax import lax
from jax.experimental import pallas as pl
from jax.experimental.pallas import tpu as pltpu

F32 = jnp.float32
BF16 = jnp.bfloat16

D_MODEL = 1024
N_META = 16
BLOCK = 128
PAD = BLOCK - N_META
EPS = 1e-6
RET_HEADS = 8
RET_DK = D_MODEL // RET_HEADS
RET_DV = 2 * D_MODEL // RET_HEADS
ROPE_BASE = 10000.0
SB_HEADS = 16
SB_HD = D_MODEL // SB_HEADS
POOL_WINDOWS = (2, 4, 8, 16)
POOL_GD = D_MODEL // len(POOL_WINDOWS)
N_EXPERTS = 8
LANES = 128
VMEM_LIMIT_BYTES = 58 * 1024 * 1024


def _cparams(*sem):
    return pltpu.CompilerParams(dimension_semantics=sem, vmem_limit_bytes=VMEM_LIMIT_BYTES)


def _pick(n, cands):
    for c in cands:
        if n % c == 0:
            return c
    raise ValueError(f"no tile for {n} in {cands}")


def _rms(x, g):
    ms = jnp.mean(x * x, axis=-1, keepdims=True)
    return x * lax.rsqrt(ms + EPS) * g


def _silu(x):
    return x * (1.0 / (1.0 + jnp.exp(-x)))


def _norm_matmul_kernel(h_ref, g_ref, w_ref, o_ref, a_scr):
    @pl.when(pl.program_id(1) == 0)
    def _():
        a_scr[...] = _rms(h_ref[...], g_ref[...]).astype(BF16)

    o_ref[...] = jnp.dot(a_scr[...], w_ref[...], preferred_element_type=F32).astype(o_ref.dtype)


def _norm_matmul(h2, g, w):
    R, D = h2.shape
    N = w.shape[1]
    tm = _pick(R, (1024, 512, 256, 128))
    tn = _pick(N, (1024, 512, 256, 128))
    return pl.pallas_call(
        _norm_matmul_kernel,
        grid=(R // tm, N // tn),
        in_specs=[pl.BlockSpec((tm, D), lambda i, j: (i, 0)),
                  pl.BlockSpec((1, D), lambda i, j: (0, 0)),
                  pl.BlockSpec((D, tn), lambda i, j: (0, j))],
        out_specs=pl.BlockSpec((tm, tn), lambda i, j: (i, j)),
        out_shape=jax.ShapeDtypeStruct((R, N), BF16),
        scratch_shapes=[pltpu.VMEM((tm, D), BF16)],
        compiler_params=_cparams("parallel", "arbitrary"),
        name="norm_matmul",
    )(h2, g.reshape(1, D), w)


def _matmul_residual_kernel(y_ref, w_ref, h_ref, o_ref):
    o_ref[...] = h_ref[...] + jnp.dot(y_ref[...], w_ref[...], preferred_element_type=F32)


def _matmul_residual(y, w, h2):
    R, K = y.shape
    D = w.shape[1]
    tm = _pick(R, (512, 256, 128))
    return pl.pallas_call(
        _matmul_residual_kernel,
        grid=(R // tm,),
        in_specs=[pl.BlockSpec((tm, K), lambda i: (i, 0)),
                  pl.BlockSpec((K, D), lambda i: (0, 0)),
                  pl.BlockSpec((tm, D), lambda i: (i, 0))],
        out_specs=pl.BlockSpec((tm, D), lambda i: (i, 0)),
        out_shape=jax.ShapeDtypeStruct((R, D), F32),
        input_output_aliases={2: 0},
        compiler_params=_cparams("parallel"),
        name="matmul_residual",
    )(y, w, h2)


def _retention_kernel(p_ref, cos_ref, sin_ref, dec_ref, zeta_ref, xi_ref, cd_ref, gn_ref, o_ref, state_scr):
    @pl.when(pl.program_id(1) == 0)
    def _():
        state_scr[...] = jnp.zeros_like(state_scr)

    cos = cos_ref[...]
    sin = sin_ref[...]
    k0, v0, g0 = D_MODEL, 2 * D_MODEL, 4 * D_MODEL
    for h in range(RET_HEADS):
        q = p_ref[:, h * RET_DK:(h + 1) * RET_DK].astype(F32)
        k = p_ref[:, k0 + h * RET_DK:k0 + (h + 1) * RET_DK].astype(F32)
        v = p_ref[:, v0 + h * RET_DV:v0 + (h + 1) * RET_DV]
        gate = p_ref[:, g0 + h * RET_DV:g0 + (h + 1) * RET_DV].astype(F32)
        qr = q * cos + pltpu.roll(q, RET_DK // 2, 1) * sin
        kr = (k * cos + pltpu.roll(k, RET_DK // 2, 1) * sin) * (RET_DK ** -0.5)
        qb = qr.astype(BF16)
        kb = kr.astype(BF16)
        s = lax.dot_general(qb, kb, (((1,), (1,)), ((), ())), preferred_element_type=F32) * dec_ref[h]
        intra = jnp.dot(s.astype(BF16), v, preferred_element_type=F32)
        st = state_scr[h]
        cross = jnp.dot(qb, st.astype(BF16), preferred_element_type=F32) * xi_ref[h]
        y = intra + cross
        kz = (kr * zeta_ref[h]).astype(BF16)
        kv = lax.dot_general(kz, v, (((0,), (0,)), ((), ())), preferred_element_type=F32)
        state_scr[h] = st * cd_ref[h] + kv
        mu = jnp.mean(y, axis=-1, keepdims=True)
        d = y - mu
        var = jnp.mean(d * d, axis=-1, keepdims=True)
        yn = d * lax.rsqrt(var + EPS) * gn_ref[:, h * RET_DV:(h + 1) * RET_DV]
        o_ref[:, h * RET_DV:(h + 1) * RET_DV] = (_silu(gate) * yn).astype(o_ref.dtype)


def _retention_tables(Lp):
    half = RET_DK // 2
    inv = ROPE_BASE ** (-jnp.arange(half, dtype=F32) / half)
    pos = (jnp.arange(Lp) - PAD).astype(F32)
    ang = pos[:, None] * inv[None, :]
    cos, sin = jnp.cos(ang), jnp.sin(ang)
    cos2 = jnp.concatenate([cos, cos], axis=-1)
    sin2 = jnp.concatenate([-sin, sin], axis=-1)
    log_g = jnp.log1p(-jnp.exp2(-5.0 - jnp.arange(RET_HEADS, dtype=F32)))
    idx = jnp.arange(BLOCK, dtype=F32)
    diff = idx[:, None] - idx[None, :]
    decay = jnp.where(diff >= 0, jnp.exp(jnp.maximum(diff, 0.0)[None] * log_g[:, None, None]), 0.0)
    zeta = jnp.exp((BLOCK - 1 - idx)[None, :] * log_g[:, None])
    xi = jnp.exp((idx + 1)[None, :] * log_g[:, None])
    cd = jnp.exp(BLOCK * log_g)
    zeta_b = jnp.broadcast_to(zeta[:, :, None], (RET_HEADS, BLOCK, RET_DK))
    xi_b = jnp.broadcast_to(xi[:, :, None], (RET_HEADS, BLOCK, RET_DV))
    cd_b = jnp.broadcast_to(cd[:, None, None], (RET_HEADS, 1, RET_DV))
    return cos2, sin2, decay, zeta_b, xi_b, cd_b


def _retention(p3, gn_g):
    B, Lp, W = p3.shape
    cos2, sin2, decay, zeta_b, xi_b, cd_b = _retention_tables(Lp)
    const3 = lambda b, c: (0, 0, 0)
    return pl.pallas_call(
        _retention_kernel,
        grid=(B, Lp // BLOCK),
        in_specs=[pl.BlockSpec((None, BLOCK, W), lambda b, c: (b, c, 0)),
                  pl.BlockSpec((BLOCK, RET_DK), lambda b, c: (c, 0)),
                  pl.BlockSpec((BLOCK, RET_DK), lambda b, c: (c, 0)),
                  pl.BlockSpec((RET_HEADS, BLOCK, BLOCK), const3),
                  pl.BlockSpec((RET_HEADS, BLOCK, RET_DK), const3),
                  pl.BlockSpec((RET_HEADS, BLOCK, RET_DV), const3),
                  pl.BlockSpec((RET_HEADS, 1, RET_DV), const3),
                  pl.BlockSpec((1, 2 * D_MODEL), lambda b, c: (0, 0))],
        out_specs=pl.BlockSpec((None, BLOCK, 2 * D_MODEL), lambda b, c: (b, c, 0)),
        out_shape=jax.ShapeDtypeStruct((B, Lp, 2 * D_MODEL), BF16),
        scratch_shapes=[pltpu.VMEM((RET_HEADS, RET_DK, RET_DV), F32)],
        compiler_params=_cparams("parallel", "arbitrary"),
        name="retention",
    )(p3, cos2, sin2, decay, zeta_b, xi_b, cd_b, gn_g.reshape(1, 2 * D_MODEL))


SB_PAIRS = D_MODEL // LANES


def _sb_kernel(q_ref, k_ref, v_ref, u_ref, o_ref, q_scr, acc_scr, car_scr):
    i = pl.program_id(1)
    lane = lax.broadcasted_iota(jnp.int32, (BLOCK, LANES), 1)
    for p in range(SB_PAIRS):
        qs = q_ref[:, p * LANES:(p + 1) * LANES] * (SB_HD ** -0.5)
        zero = jnp.zeros_like(qs)
        q_scr[p, :BLOCK] = jnp.where(lane < SB_HD, qs, zero)
        q_scr[p, BLOCK:] = jnp.where(lane >= SB_HD, qs, zero)
    row = lax.broadcasted_iota(jnp.int32, (2 * BLOCK, BLOCK), 0)
    col = lax.broadcasted_iota(jnp.int32, (2 * BLOCK, BLOCK), 1)
    qpos = i * BLOCK + (row & (BLOCK - 1))
    acc_scr[...] = jnp.zeros_like(acc_scr)
    car_scr[...] = jnp.zeros_like(car_scr)

    def key_block(t, carry):
        j = i - t
        rows = pl.ds(pl.multiple_of(j * BLOCK, BLOCK), BLOCK)
        kpos = j * BLOCK + col
        mask = (kpos < qpos) & (kpos >= PAD)
        pairs = range(SB_PAIRS)
        cols = [slice(p * LANES, (p + 1) * LANES) for p in pairs]
        z = [lax.dot_general(q_scr[p], k_ref[rows, cols[p]], (((1,), (1,)), ((), ())),
                             preferred_element_type=F32) for p in pairs]
        log_beta = [jnp.minimum(z[p], 0.0) - jnp.log(1.0 + jnp.exp(-jnp.abs(z[p]))) for p in pairs]
        log_1m = [jnp.where(mask, log_beta[p] - z[p], 0.0).astype(BF16) for p in pairs]
        r = [jnp.dot(log_1m[p], u_ref[...], preferred_element_type=F32) for p in pairs]
        w = [jnp.where(mask, jnp.exp(log_beta[p] + (car_scr[p] + r[p][:, :BLOCK])), 0.0).astype(BF16)
             for p in pairs]
        o = [jnp.dot(w[p], v_ref[rows, cols[p]], preferred_element_type=F32) for p in pairs]
        for p in pairs:
            acc_scr[p] += o[p]
            car_scr[p] += r[p][:, BLOCK:]
        return carry

    lax.fori_loop(0, i + 1, key_block, 0)
    for p in range(SB_PAIRS):
        o_ref[:, p * LANES:(p + 1) * LANES] = jnp.where(
            lane < SB_HD, acc_scr[p, :BLOCK], acc_scr[p, BLOCK:]).astype(o_ref.dtype)


def _stick_breaking(qkv3):
    B, Lp, _ = qkv3.shape
    j_idx = np.arange(BLOCK)
    u = np.concatenate([(j_idx[:, None] > j_idx[None, :]).astype(np.float32),
                        np.ones((BLOCK, BLOCK), np.float32)], axis=1)
    return pl.pallas_call(
        _sb_kernel,
        grid=(B, Lp // BLOCK),
        in_specs=[pl.BlockSpec((None, BLOCK, D_MODEL), lambda b, i: (b, i, 0)),
                  pl.BlockSpec((None, Lp, D_MODEL), lambda b, i: (b, 0, 1)),
                  pl.BlockSpec((None, Lp, D_MODEL), lambda b, i: (b, 0, 2)),
                  pl.BlockSpec((BLOCK, 2 * BLOCK), lambda b, i: (0, 0))],
        out_specs=pl.BlockSpec((None, BLOCK, D_MODEL), lambda b, i: (b, i, 0)),
        out_shape=jax.ShapeDtypeStruct((B, Lp, D_MODEL), BF16),
        scratch_shapes=[pltpu.VMEM((SB_PAIRS, 2 * BLOCK, LANES), BF16),
                        pltpu.VMEM((SB_PAIRS, 2 * BLOCK, LANES), F32),
                        pltpu.VMEM((SB_PAIRS, 2 * BLOCK, BLOCK), F32)],
        compiler_params=_cparams("parallel", "arbitrary"),
        name="stick_breaking",
    )(qkv3, qkv3, qkv3, jnp.asarray(u, BF16))


def _pool_kernel(h_ref, g_ref, bm_ref, bh_ref, w_ref, sc_ref, o_ref, prev_scr):
    j = pl.program_id(1)

    @pl.when(j == 0)
    def _():
        prev_scr[...] = jnp.zeros_like(prev_scr)

    x = h_ref[...]
    pos = j * BLOCK + lax.broadcasted_iota(jnp.int32, (BLOCK, POOL_GD), 0) - PAD
    a = _rms(x, g_ref[...])
    a_hi = a.astype(BF16)
    a_lo = (a - a_hi.astype(F32)).astype(BF16)
    for g, win in enumerate(POOL_WINDOWS):
        sl = slice(g * POOL_GD, (g + 1) * POOL_GD)
        ws = (jnp.dot(bm_ref[g], a_hi[:, sl], preferred_element_type=F32)
              + jnp.dot(bm_ref[g], a_lo[:, sl], preferred_element_type=F32)
              + jnp.dot(bh_ref[g], prev_scr[0, :, sl], preferred_element_type=F32)
              + jnp.dot(bh_ref[g], prev_scr[1, :, sl], preferred_element_type=F32))
        cnt = jnp.clip(pos + 1, 1, win).astype(F32)
        diff = ws / cnt - a[:, sl]
        y = jnp.dot(diff.astype(BF16), w_ref[g], preferred_element_type=F32) * sc_ref[:, sl]
        o_ref[:, sl] = x[:, sl] + y
    prev_scr[0] = a_hi
    prev_scr[1] = a_lo


def _pool_mixer(h3, g, w_pool, scale):
    B, Lp, D = h3.shape
    t = np.arange(BLOCK)
    d_main = t[:, None] - t[None, :]
    d_halo = t[:, None] + BLOCK - t[None, :]
    bm = np.stack([((d_main >= 0) & (d_main < w)) for w in POOL_WINDOWS]).astype(np.float32)
    bh = np.stack([(d_halo < w) for w in POOL_WINDOWS]).astype(np.float32)
    ng = len(POOL_WINDOWS)
    return pl.pallas_call(
        _pool_kernel,
        grid=(B, Lp // BLOCK),
        in_specs=[pl.BlockSpec((None, BLOCK, D), lambda b, j: (b, j, 0)),
                  pl.BlockSpec((1, D), lambda b, j: (0, 0)),
                  pl.BlockSpec((ng, BLOCK, BLOCK), lambda b, j: (0, 0, 0)),
                  pl.BlockSpec((ng, BLOCK, BLOCK), lambda b, j: (0, 0, 0)),
                  pl.BlockSpec((ng, POOL_GD, POOL_GD), lambda b, j: (0, 0, 0)),
                  pl.BlockSpec((1, D), lambda b, j: (0, 0))],
        out_specs=pl.BlockSpec((None, BLOCK, D), lambda b, j: (b, j, 0)),
        out_shape=jax.ShapeDtypeStruct((B, Lp, D), F32),
        scratch_shapes=[pltpu.VMEM((2, BLOCK, D), BF16)],
        input_output_aliases={0: 0},
        compiler_params=_cparams("parallel", "arbitrary"),
        name="pool_mixer",
    )(h3, g.reshape(1, D), jnp.asarray(bm, BF16), jnp.asarray(bh, BF16), w_pool.astype(BF16),
      scale.reshape(1, D))


def _ffn_kernel(te_ref, nu_ref, x_ref, g_ref, w1_ref, w3_ref, w2_ref, o_ref, a_scr, *, fused, fc):
    t = pl.program_id(0)

    @pl.when(t < nu_ref[0])
    def _():
        x = x_ref[...]
        a_scr[...] = (_rms(x, g_ref[...]) if fused else x).astype(BF16)
        for c in range(w1_ref.shape[-1] // fc):
            cs = slice(c * fc, (c + 1) * fc)
            a = a_scr[...]
            h1 = jnp.dot(a, w1_ref[0, :, cs], preferred_element_type=F32)
            h3 = jnp.dot(a, w3_ref[0, :, cs], preferred_element_type=F32)
            y = jnp.dot((_silu(h1) * h3).astype(BF16), w2_ref[0, cs, :], preferred_element_type=F32)
            if c == 0:
                o_ref[...] = x + y if fused else y
            else:
                o_ref[...] += y

    @pl.when(t >= nu_ref[0])
    def _():
        o_ref[...] = jnp.zeros_like(o_ref)


def _ffn(x2, g, w1, w3, w2, tile_expert, n_used, *, fused):
    R, D = x2.shape
    F = w1.shape[-1]
    tm = _pick(R, (512, 256, 128))
    kern = functools.partial(_ffn_kernel, fused=fused, fc=256)
    grid_spec = pltpu.PrefetchScalarGridSpec(
        num_scalar_prefetch=2,
        grid=(R // tm,),
        in_specs=[pl.BlockSpec((tm, D), lambda t, te, nu: (t, 0)),
                  pl.BlockSpec((1, D), lambda t, te, nu: (0, 0)),
                  pl.BlockSpec((1, D, F), lambda t, te, nu: (te[t], 0, 0)),
                  pl.BlockSpec((1, D, F), lambda t, te, nu: (te[t], 0, 0)),
                  pl.BlockSpec((1, F, D), lambda t, te, nu: (te[t], 0, 0))],
        out_specs=pl.BlockSpec((tm, D), lambda t, te, nu: (t, 0)),
        scratch_shapes=[pltpu.VMEM((tm, D), BF16)])
    return pl.pallas_call(
        kern,
        grid_spec=grid_spec,
        out_shape=jax.ShapeDtypeStruct((R, D), F32),
        input_output_aliases={2: 0} if fused else {},
        compiler_params=_cparams("arbitrary"),
        name="swiglu_fused" if fused else "swiglu_experts",
    )(tile_expert, n_used, x2, g.reshape(1, D), w1, w3, w2)


def _router_kernel(h_ref, g_ref, wr_ref, tri_ref, slab_ref, cnt_ref, carry_scr):
    first = (pl.program_id(0) == 0) & (pl.program_id(1) == 0)

    @pl.when(first)
    def _():
        carry_scr[...] = jnp.zeros_like(carry_scr)

    tr = h_ref.shape[0]
    a = _rms(h_ref[...], g_ref[...])
    a_hi = a.astype(BF16)
    a_lo = (a - a_hi.astype(F32)).astype(BF16)
    wr = wr_ref[...]
    w_hi = wr.astype(BF16)
    w_lo = (wr - w_hi.astype(F32)).astype(BF16)
    logits = (jnp.dot(a_hi, w_hi, preferred_element_type=F32)
              + jnp.dot(a_hi, w_lo, preferred_element_type=F32)
              + jnp.dot(a_lo, w_hi, preferred_element_type=F32))
    lane = lax.broadcasted_iota(jnp.int32, (tr, LANES), 1).astype(F32)
    neg = jnp.float32(-jnp.inf)
    logits = jnp.where(lane < N_EXPERTS, logits, neg)
    m1 = jnp.max(logits, axis=-1, keepdims=True)
    i1 = jnp.min(jnp.where(logits == m1, lane, float(LANES)), axis=-1, keepdims=True)
    rest = jnp.where(lane == i1, neg, logits)
    m2 = jnp.max(rest, axis=-1, keepdims=True)
    i2 = jnp.min(jnp.where(rest == m2, lane, float(LANES)), axis=-1, keepdims=True)
    e = jnp.exp(m2 - m1)
    g1 = 1.0 / (1.0 + e)
    g2 = e / (1.0 + e)
    pos = pl.program_id(1) * tr + lax.broadcasted_iota(jnp.int32, (tr, LANES), 0)
    valid = (pos >= PAD).astype(F32)
    oh1 = (lane == i1).astype(F32)
    oh2 = (lane == i2).astype(F32)
    chosen = (oh1 + oh2) * valid
    before = carry_scr[0:1, :] + jnp.dot(tri_ref[...], chosen.astype(BF16), preferred_element_type=F32)
    r1 = jnp.sum(before * oh1, axis=-1, keepdims=True)
    r2 = jnp.sum(before * oh2, axis=-1, keepdims=True)
    total = carry_scr[0:1, :] + jnp.sum(chosen, axis=0, keepdims=True)
    carry_scr[...] = jnp.broadcast_to(total, carry_scr.shape)
    cnt_ref[...] = jnp.broadcast_to(total, cnt_ref.shape)
    cols = (i1, i2, g1 * valid, g2 * valid, r1, r2)
    slab = jnp.zeros((tr, LANES), F32)
    for c, val in enumerate(cols):
        slab = jnp.where(lane == c, val, slab)
    slab_ref[...] = slab


def _router(h3, g, w_router):
    B, Lp, D = h3.shape
    tr = Lp // 2
    wr = jnp.zeros((D, LANES), F32).at[:, :N_EXPERTS].set(w_router)
    t = np.arange(tr)
    tri = (t[:, None] > t[None, :]).astype(np.float32)
    slab, cnt = pl.pallas_call(
        _router_kernel,
        grid=(B, Lp // tr),
        in_specs=[pl.BlockSpec((None, tr, D), lambda b, j: (b, j, 0)),
                  pl.BlockSpec((1, D), lambda b, j: (0, 0)),
                  pl.BlockSpec((D, LANES), lambda b, j: (0, 0)),
                  pl.BlockSpec((tr, tr), lambda b, j: (0, 0))],
        out_specs=[pl.BlockSpec((tr, LANES), lambda b, j: (b * (Lp // tr) + j, 0)),
                   pl.BlockSpec((8, LANES), lambda b, j: (0, 0))],
        out_shape=[jax.ShapeDtypeStruct((B * Lp, LANES), F32), jax.ShapeDtypeStruct((8, LANES), F32)],
        scratch_shapes=[pltpu.VMEM((8, LANES), F32)],
        compiler_params=_cparams("arbitrary", "arbitrary"),
        name="moe_router",
    )(h3, g.reshape(1, D), wr, jnp.asarray(tri, BF16))
    return slab, cnt[0, :N_EXPERTS]


def _dispatch_kernel(p1_ref, p2_ref, h_ref, g_ref, xs_in_ref, xs_ref, a_scr, sem):
    del xs_in_ref
    b = pl.program_id(0)
    j = pl.program_id(1)
    nb = pl.num_programs(0)
    td = h_ref.shape[0]
    first_row = (PAD, 0)

    def wait_slot(s):
        n = td - first_row[s]
        for _ in range(2):
            pltpu.make_async_copy(a_scr.at[s, pl.ds(0, n)], xs_ref.at[pl.ds(0, n)], sem.at[s]).wait()

    for s in range(2):
        @pl.when((j == s) & (b > 0))
        def _(s=s):
            wait_slot(s)

    a = _rms(h_ref[...], g_ref[...])
    for s in range(2):
        @pl.when(j == s)
        def _(s=s):
            a_scr[s] = a
            base = (b * 2 + s) * td

            def row(r, carry):
                src = a_scr.at[s, pl.ds(r, 1)]
                pltpu.make_async_copy(src, xs_ref.at[pl.ds(p1_ref[base + r], 1)], sem.at[s]).start()
                pltpu.make_async_copy(src, xs_ref.at[pl.ds(p2_ref[base + r], 1)], sem.at[s]).start()
                return carry

            lax.fori_loop(first_row[s], td, row, 0, unroll=8)

    @pl.when((b == nb - 1) & (j == 1))
    def _():
        wait_slot(0)
        wait_slot(1)


def _dispatch(h3, g, p1, p2, n_rows):
    B, Lp, D = h3.shape
    td = Lp // 2
    grid_spec = pltpu.PrefetchScalarGridSpec(
        num_scalar_prefetch=2,
        grid=(B, 2),
        in_specs=[pl.BlockSpec((None, td, D), lambda b, j, p1, p2: (b, j, 0)),
                  pl.BlockSpec((1, D), lambda b, j, p1, p2: (0, 0)),
                  pl.BlockSpec(memory_space=pl.ANY)],
        out_specs=pl.BlockSpec(memory_space=pl.ANY),
        scratch_shapes=[pltpu.VMEM((2, td, D), F32), pltpu.SemaphoreType.DMA((2,))])
    return pl.pallas_call(
        _dispatch_kernel,
        grid_spec=grid_spec,
        out_shape=jax.ShapeDtypeStruct((n_rows, D), F32),
        input_output_aliases={4: 0},
        compiler_params=_cparams("arbitrary", "arbitrary"),
        name="moe_dispatch",
    )(p1, p2, h3, g.reshape(1, D), jnp.zeros((n_rows, D), F32))


def _combine_kernel(p1_ref, p2_ref, h_ref, slab_ref, y_ref, o_ref, buf, sem):
    i = pl.program_id(0)
    n = pl.num_programs(0)
    tc = h_ref.shape[0]

    def start(step, slot):
        base = step * tc

        def row(r, carry):
            pltpu.make_async_copy(y_ref.at[pl.ds(p1_ref[base + r], 1)], buf.at[slot, 0, pl.ds(r, 1)],
                                  sem.at[slot]).start()
            pltpu.make_async_copy(y_ref.at[pl.ds(p2_ref[base + r], 1)], buf.at[slot, 1, pl.ds(r, 1)],
                                  sem.at[slot]).start()
            return carry

        lax.fori_loop(0, tc, row, 0, unroll=8)

    @pl.when(i == 0)
    def _():
        start(0, 0)

    for s in range(2):
        @pl.when((i + 1 < n) & ((i + 1) % 2 == s))
        def _(s=s):
            start(i + 1, s)

    for s in range(2):
        @pl.when(i % 2 == s)
        def _(s=s):
            for c in range(2):
                pltpu.make_async_copy(y_ref.at[pl.ds(0, tc)], buf.at[s, c], sem.at[s]).wait()
            slab = slab_ref[...]
            o_ref[...] = h_ref[...] + (slab[:, 2:3] * buf[s, 0] + slab[:, 3:4] * buf[s, 1])


def _combine(h2, slab, y, p1, p2):
    R, D = h2.shape
    tc = _pick(R, (256, 128))
    grid_spec = pltpu.PrefetchScalarGridSpec(
        num_scalar_prefetch=2,
        grid=(R // tc,),
        in_specs=[pl.BlockSpec((tc, D), lambda i, p1, p2: (i, 0)),
                  pl.BlockSpec((tc, LANES), lambda i, p1, p2: (i, 0)),
                  pl.BlockSpec(memory_space=pl.ANY)],
        out_specs=pl.BlockSpec((tc, D), lambda i, p1, p2: (i, 0)),
        scratch_shapes=[pltpu.VMEM((2, 2, tc, D), F32), pltpu.SemaphoreType.DMA((2,))])
    return pl.pallas_call(
        _combine_kernel,
        grid_spec=grid_spec,
        out_shape=jax.ShapeDtypeStruct((R, D), F32),
        input_output_aliases={2: 0},
        compiler_params=_cparams("arbitrary"),
        name="moe_combine",
    )(p1, p2, h2, slab, y)


def _moe(h3, g, w_router, w1, w3, w2):
    B, Lp, D = h3.shape
    tm = 512
    n_tokens = B * (Lp - PAD)
    n_tiles = (2 * n_tokens) // tm + N_EXPERTS
    slab, counts = _router(h3, g, w_router)
    counts = counts.astype(jnp.int32)
    tiles_e = (counts + tm - 1) // tm
    end_tile = jnp.cumsum(tiles_e)
    offset = ((end_tile - tiles_e) * tm).astype(F32)
    n_used = end_tile[-1:]
    expert_ids = jnp.arange(N_EXPERTS, dtype=F32)
    e1, e2 = slab[:, 0:1], slab[:, 1:2]
    valid = (jnp.arange(B * Lp) % Lp) >= PAD
    off1 = jnp.sum(jnp.where(e1 == expert_ids[None, :], offset[None, :], 0.0), axis=-1)
    off2 = jnp.sum(jnp.where(e2 == expert_ids[None, :], offset[None, :], 0.0), axis=-1)
    p1 = jnp.where(valid, off1 + slab[:, 4], 0.0).astype(jnp.int32)
    p2 = jnp.where(valid, off2 + slab[:, 5], 0.0).astype(jnp.int32)
    tile_ids = jnp.arange(n_tiles, dtype=jnp.int32)
    last_tile = jnp.maximum(n_used - 1, 0)
    tile_expert = jnp.sum(jnp.minimum(tile_ids, last_tile)[:, None] >= end_tile[None, :], axis=-1).astype(jnp.int32)
    tile_expert = jnp.minimum(tile_expert, N_EXPERTS - 1)
    xs = _dispatch(h3, g, p1, p2, n_tiles * tm)
    ys = _ffn(xs, g, w1, w3, w2, tile_expert, n_used.astype(jnp.int32), fused=False)
    return _combine(h3.reshape(B * Lp, D), slab, ys, p1, p2).reshape(B, Lp, D)


def _final_norm_kernel(h_ref, g_ref, o_ref):
    o_ref[...] = _rms(h_ref[...], g_ref[...])


def _final_norm(h3, g):
    B, Lp, D = h3.shape
    lead = (PAD + N_META) // BLOCK
    n = Lp // BLOCK - lead
    return pl.pallas_call(
        _final_norm_kernel,
        grid=(B, n),
        in_specs=[pl.BlockSpec((None, BLOCK, D), lambda b, j: (b, j + lead, 0)),
                  pl.BlockSpec((1, D), lambda b, j: (0, 0))],
        out_specs=pl.BlockSpec((None, BLOCK, D), lambda b, j: (b, j, 0)),
        out_shape=jax.ShapeDtypeStruct((B, n * BLOCK, D), F32),
        compiler_params=_cparams("parallel", "parallel"),
        name="final_norm",
    )(h3, g.reshape(1, D))


def kernel(x, meta_tokens, norm_mix_g, norm_ffn_g, ret_wq, ret_wk, ret_wv, ret_wg, ret_wo, ret_gn_g,
           sb_wqkv, sb_wo, pool_w, pool_scale, ffn_w1, ffn_w3, ffn_w2,
           moe_router, moe_w1, moe_w3, moe_w2, final_norm_g):
    B, S, D = x.shape
    Lp = PAD + N_META + S
    R = B * Lp
    depth = norm_mix_g.shape[0]
    meta = jnp.broadcast_to(meta_tokens.astype(x.dtype)[None], (B, N_META, D))
    h = jnp.concatenate([jnp.zeros((B, PAD, D), x.dtype), meta, x], axis=1)
    dense_tiles = jnp.zeros((R // _pick(R, (512, 256, 128)),), jnp.int32)
    all_tiles = jnp.full((1,), dense_tiles.shape[0], jnp.int32)
    for i in range(depth):
        m, j = i % 3, i // 3
        if m == 0:
            w_in = jnp.concatenate([ret_wq[j], ret_wk[j], ret_wv[j], ret_wg[j]], axis=1).astype(BF16)
            p = _norm_matmul(h.reshape(R, D), norm_mix_g[i], w_in)
            y = _retention(p.reshape(B, Lp, -1), ret_gn_g[j])
            h = _matmul_residual(y.reshape(R, -1), ret_wo[j].astype(BF16), h.reshape(R, D)).reshape(B, Lp, D)
        elif m == 1:
            qkv = _norm_matmul(h.reshape(R, D), norm_mix_g[i], sb_wqkv[j].astype(BF16))
            y = _stick_breaking(qkv.reshape(B, Lp, -1))
            h = _matmul_residual(y.reshape(R, D), sb_wo[j].astype(BF16), h.reshape(R, D)).reshape(B, Lp, D)
        else:
            h = _pool_mixer(h, norm_mix_g[i], pool_w[j], pool_scale[j])
        c = i // 2
        if i % 2 == 0:
            h = _ffn(h.reshape(R, D), norm_ffn_g[i], ffn_w1[c:c + 1].astype(BF16), ffn_w3[c:c + 1].astype(BF16),
                     ffn_w2[c:c + 1].astype(BF16), dense_tiles, all_tiles, fused=True).reshape(B, Lp, D)
        else:
            h = _moe(h, norm_ffn_g[i], moe_router[c], moe_w1[c].astype(BF16), moe_w3[c].astype(BF16),
                     moe_w2[c].astype(BF16))
    return _final_norm(h, final_norm_g)
```

```python
import functools

import numpy as np
import jax
import jax.numpy as jnp
from jax import lax
from jax.experimental import pallas as pl
from jax.experimental.pallas import tpu as pltpu

F32 = jnp.float32
BF16 = jnp.bfloat16

D_MODEL = 1024
N_META = 16
BLOCK = 128
PAD = BLOCK - N_META
EPS = 1e-6
RET_HEADS = 8
RET_DK = D_MODEL // RET_HEADS
RET_DV = 2 * D_MODEL // RET_HEADS
ROPE_BASE = 10000.0
SB_HEADS = 16
SB_HD = D_MODEL // SB_HEADS
POOL_WINDOWS = (2, 4, 8, 16)
POOL_GD = D_MODEL // len(POOL_WINDOWS)
N_EXPERTS = 8
LANES = 128
VMEM_LIMIT_BYTES = 58 * 1024 * 1024


def _cparams(*sem):
    return pltpu.CompilerParams(dimension_semantics=sem, vmem_limit_bytes=VMEM_LIMIT_BYTES)


def _pick(n, cands):
    for c in cands:
        if n % c == 0:
            return c
    raise ValueError(f"no tile for {n} in {cands}")


def _rms(x, g):
    ms = jnp.mean(x * x, axis=-1, keepdims=True)
    return x * lax.rsqrt(ms + EPS) * g


def _silu(x):
    return x * (1.0 / (1.0 + jnp.exp(-x)))


def _norm_matmul_kernel(h_ref, g_ref, w_ref, o_ref, a_scr):
    @pl.when(pl.program_id(1) == 0)
    def _():
        a_scr[...] = _rms(h_ref[...], g_ref[...]).astype(BF16)

    o_ref[...] = jnp.dot(a_scr[...], w_ref[...], preferred_element_type=F32).astype(o_ref.dtype)


def _norm_matmul(h2, g, w):
    R, D = h2.shape
    N = w.shape[1]
    tm = _pick(R, (1024, 512, 256, 128))
    tn = _pick(N, (1024, 512, 256, 128))
    return pl.pallas_call(
        _norm_matmul_kernel,
        grid=(R // tm, N // tn),
        in_specs=[pl.BlockSpec((tm, D), lambda i, j: (i, 0)),
                  pl.BlockSpec((1, D), lambda i, j: (0, 0)),
                  pl.BlockSpec((D, tn), lambda i, j: (0, j))],
        out_specs=pl.BlockSpec((tm, tn), lambda i, j: (i, j)),
        out_shape=jax.ShapeDtypeStruct((R, N), BF16),
        scratch_shapes=[pltpu.VMEM((tm, D), BF16)],
        compiler_params=_cparams("parallel", "arbitrary"),
        name="norm_matmul",
    )(h2, g.reshape(1, D), w)


def _matmul_residual_kernel(y_ref, w_ref, h_ref, o_ref):
    o_ref[...] = h_ref[...] + jnp.dot(y_ref[...], w_ref[...], preferred_element_type=F32)


def _matmul_residual(y, w, h2):
    R, K = y.shape
    D = w.shape[1]
    tm = _pick(R, (512, 256, 128))
    return pl.pallas_call(
        _matmul_residual_kernel,
        grid=(R // tm,),
        in_specs=[pl.BlockSpec((tm, K), lambda i: (i, 0)),
                  pl.BlockSpec((K, D), lambda i: (0, 0)),
                  pl.BlockSpec((tm, D), lambda i: (i, 0))],
        out_specs=pl.BlockSpec((tm, D), lambda i: (i, 0)),
        out_shape=jax.ShapeDtypeStruct((R, D), F32),
        input_output_aliases={2: 0},
        compiler_params=_cparams("parallel"),
        name="matmul_residual",
    )(y, w, h2)


def _retention_kernel(p_ref, cos_ref, sin_ref, dec_ref, zeta_ref, xi_ref, cd_ref, gn_ref, o_ref, state_scr):
    @pl.when(pl.program_id(1) == 0)
    def _():
        state_scr[...] = jnp.zeros_like(state_scr)

    cos = cos_ref[...]
    sin = sin_ref[...]
    k0, v0, g0 = D_MODEL, 2 * D_MODEL, 4 * D_MODEL
    for h in range(RET_HEADS):
        q = p_ref[:, h * RET_DK:(h + 1) * RET_DK].astype(F32)
        k = p_ref[:, k0 + h * RET_DK:k0 + (h + 1) * RET_DK].astype(F32)
        v = p_ref[:, v0 + h * RET_DV:v0 + (h + 1) * RET_DV]
        gate = p_ref[:, g0 + h * RET_DV:g0 + (h + 1) * RET_DV].astype(F32)
        qr = q * cos + pltpu.roll(q, RET_DK // 2, 1) * sin
        kr = (k * cos + pltpu.roll(k, RET_DK // 2, 1) * sin) * (RET_DK ** -0.5)
        qb = qr.astype(BF16)
        kb = kr.astype(BF16)
        s = lax.dot_general(qb, kb, (((1,), (1,)), ((), ())), preferred_element_type=F32) * dec_ref[h]
        intra = jnp.dot(s.astype(BF16), v, preferred_element_type=F32)
        st = state_scr[h]
        cross = jnp.dot(qb, st.astype(BF16), preferred_element_type=F32) * xi_ref[h]
        y = intra + cross
        kz = (kr * zeta_ref[h]).astype(BF16)
        kv = lax.dot_general(kz, v, (((0,), (0,)), ((), ())), preferred_element_type=F32)
        state_scr[h] = st * cd_ref[h] + kv
        mu = jnp.mean(y, axis=-1, keepdims=True)
        d = y - mu
        var = jnp.mean(d * d, axis=-1, keepdims=True)
        yn = d * lax.rsqrt(var + EPS) * gn_ref[:, h * RET_DV:(h + 1) * RET_DV]
        o_ref[:, h * RET_DV:(h + 1) * RET_DV] = (_silu(gate) * yn).astype(o_ref.dtype)


def _retention_tables(Lp):
    half = RET_DK // 2
    inv = ROPE_BASE ** (-jnp.arange(half, dtype=F32) / half)
    pos = (jnp.arange(Lp) - PAD).astype(F32)
    ang = pos[:, None] * inv[None, :]
    cos, sin = jnp.cos(ang), jnp.sin(ang)
    cos2 = jnp.concatenate([cos, cos], axis=-1)
    sin2 = jnp.concatenate([-sin, sin], axis=-1)
    log_g = jnp.log1p(-jnp.exp2(-5.0 - jnp.arange(RET_HEADS, dtype=F32)))
    idx = jnp.arange(BLOCK, dtype=F32)
    diff = idx[:, None] - idx[None, :]
    decay = jnp.where(diff >= 0, jnp.exp(jnp.maximum(diff, 0.0)[None] * log_g[:, None, None]), 0.0)
    zeta = jnp.exp((BLOCK - 1 - idx)[None, :] * log_g[:, None])
    xi = jnp.exp((idx + 1)[None, :] * log_g[:, None])
    cd = jnp.exp(BLOCK * log_g)
    zeta_b = jnp.broadcast_to(zeta[:, :, None], (RET_HEADS, BLOCK, RET_DK))
    xi_b = jnp.broadcast_to(xi[:, :, None], (RET_HEADS, BLOCK, RET_DV))
    cd_b = jnp.broadcast_to(cd[:, None, None], (RET_HEADS, 1, RET_DV))
    return cos2, sin2, decay, zeta_b, xi_b, cd_b


def _retention(p3, gn_g):
    B, Lp, W = p3.shape
    cos2, sin2, decay, zeta_b, xi_b, cd_b = _retention_tables(Lp)
    const3 = lambda b, c: (0, 0, 0)
    return pl.pallas_call(
        _retention_kernel,
        grid=(B, Lp // BLOCK),
        in_specs=[pl.BlockSpec((None, BLOCK, W), lambda b, c: (b, c, 0)),
                  pl.BlockSpec((BLOCK, RET_DK), lambda b, c: (c, 0)),
                  pl.BlockSpec((BLOCK, RET_DK), lambda b, c: (c, 0)),
                  pl.BlockSpec((RET_HEADS, BLOCK, BLOCK), const3),
                  pl.BlockSpec((RET_HEADS, BLOCK, RET_DK), const3),
                  pl.BlockSpec((RET_HEADS, BLOCK, RET_DV), const3),
                  pl.BlockSpec((RET_HEADS, 1, RET_DV), const3),
                  pl.BlockSpec((1, 2 * D_MODEL), lambda b, c: (0, 0))],
        out_specs=pl.BlockSpec((None, BLOCK, 2 * D_MODEL), lambda b, c: (b, c, 0)),
        out_shape=jax.ShapeDtypeStruct((B, Lp, 2 * D_MODEL), BF16),
        scratch_shapes=[pltpu.VMEM((RET_HEADS, RET_DK, RET_DV), F32)],
        compiler_params=_cparams("parallel", "arbitrary"),
        name="retention",
    )(p3, cos2, sin2, decay, zeta_b, xi_b, cd_b, gn_g.reshape(1, 2 * D_MODEL))


SB_PAIRS = D_MODEL // LANES


def _sb_kernel(q_ref, k_ref, v_ref, u_ref, o_ref, q_scr, acc_scr, car_scr):
    i = pl.program_id(1)
    lane = lax.broadcasted_iota(jnp.int32, (BLOCK, LANES), 1)
    for p in range(SB_PAIRS):
        qs = q_ref[:, p * LANES:(p + 1) * LANES] * (SB_HD ** -0.5)
        zero = jnp.zeros_like(qs)
        q_scr[p, :BLOCK] = jnp.where(lane < SB_HD, qs, zero)
        q_scr[p, BLOCK:] = jnp.where(lane >= SB_HD, qs, zero)
    row = lax.broadcasted_iota(jnp.int32, (2 * BLOCK, BLOCK), 0)
    col = lax.broadcasted_iota(jnp.int32, (2 * BLOCK, BLOCK), 1)
    qpos = i * BLOCK + (row & (BLOCK - 1))
    acc_scr[...] = jnp.zeros_like(acc_scr)
    car_scr[...] = jnp.zeros_like(car_scr)

    def key_block(j, masked):
        rows = pl.ds(pl.multiple_of(j * BLOCK, BLOCK), BLOCK)
        if masked:
            kpos = j * BLOCK + col
            mask = (kpos < qpos) & (kpos >= PAD)
            sel = lambda x: jnp.where(mask, x, 0.0)
        else:
            sel = lambda x: x
        pairs = range(SB_PAIRS)
        cols = [slice(p * LANES, (p + 1) * LANES) for p in pairs]
        z = [lax.dot_general(q_scr[p], k_ref[rows, cols[p]], (((1,), (1,)), ((), ())),
                             preferred_element_type=F32) for p in pairs]
        log_beta = [jnp.minimum(z[p], 0.0) - jnp.log(1.0 + jnp.exp(-jnp.abs(z[p]))) for p in pairs]
        log_1m = [sel(log_beta[p] - z[p]).astype(BF16) for p in pairs]
        r = [jnp.dot(log_1m[p], u_ref[...], preferred_element_type=F32) for p in pairs]
        w = [sel(jnp.exp(log_beta[p] + (car_scr[p] + r[p][:, :BLOCK]))).astype(BF16) for p in pairs]
        o = [jnp.dot(w[p], v_ref[rows, cols[p]], preferred_element_type=F32) for p in pairs]
        for p in pairs:
            acc_scr[p] += o[p]
            car_scr[p] += r[p][:, BLOCK:]

    key_block(i, True)

    def interior(t, carry):
        key_block(i - 1 - t, False)
        return carry

    lax.fori_loop(0, i - 1, interior, 0)

    @pl.when(i > 0)
    def _():
        key_block(0, True)
    for p in range(SB_PAIRS):
        o_ref[:, p * LANES:(p + 1) * LANES] = jnp.where(
            lane < SB_HD, acc_scr[p, :BLOCK], acc_scr[p, BLOCK:]).astype(o_ref.dtype)


def _stick_breaking(qkv3):
    B, Lp, _ = qkv3.shape
    j_idx = np.arange(BLOCK)
    u = np.concatenate([(j_idx[:, None] > j_idx[None, :]).astype(np.float32),
                        np.ones((BLOCK, BLOCK), np.float32)], axis=1)
    return pl.pallas_call(
        _sb_kernel,
        grid=(B, Lp // BLOCK),
        in_specs=[pl.BlockSpec((None, BLOCK, D_MODEL), lambda b, i: (b, i, 0)),
                  pl.BlockSpec((None, Lp, D_MODEL), lambda b, i: (b, 0, 1)),
                  pl.BlockSpec((None, Lp, D_MODEL), lambda b, i: (b, 0, 2)),
                  pl.BlockSpec((BLOCK, 2 * BLOCK), lambda b, i: (0, 0))],
        out_specs=pl.BlockSpec((None, BLOCK, D_MODEL), lambda b, i: (b, i, 0)),
        out_shape=jax.ShapeDtypeStruct((B, Lp, D_MODEL), BF16),
        scratch_shapes=[pltpu.VMEM((SB_PAIRS, 2 * BLOCK, LANES), BF16),
                        pltpu.VMEM((SB_PAIRS, 2 * BLOCK, LANES), F32),
                        pltpu.VMEM((SB_PAIRS, 2 * BLOCK, BLOCK), F32)],
        compiler_params=_cparams("parallel", "arbitrary"),
        name="stick_breaking",
    )(qkv3, qkv3, qkv3, jnp.asarray(u, BF16))


def _pool_kernel(h_ref, g_ref, bm_ref, bh_ref, w_ref, sc_ref, o_ref, prev_scr):
    j = pl.program_id(1)

    @pl.when(j == 0)
    def _():
        prev_scr[...] = jnp.zeros_like(prev_scr)

    x = h_ref[...]
    pos = j * BLOCK + lax.broadcasted_iota(jnp.int32, (BLOCK, POOL_GD), 0) - PAD
    a = _rms(x, g_ref[...])
    a_hi = a.astype(BF16)
    a_lo = (a - a_hi.astype(F32)).astype(BF16)
    for g, win in enumerate(POOL_WINDOWS):
        sl = slice(g * POOL_GD, (g + 1) * POOL_GD)
        ws = (jnp.dot(bm_ref[g], a_hi[:, sl], preferred_element_type=F32)
              + jnp.dot(bm_ref[g], a_lo[:, sl], preferred_element_type=F32)
              + jnp.dot(bh_ref[g], prev_scr[0, :, sl], preferred_element_type=F32)
              + jnp.dot(bh_ref[g], prev_scr[1, :, sl], preferred_element_type=F32))
        cnt = jnp.clip(pos + 1, 1, win).astype(F32)
        diff = ws / cnt - a[:, sl]
        y = jnp.dot(diff.astype(BF16), w_ref[g], preferred_element_type=F32) * sc_ref[:, sl]
        o_ref[:, sl] = x[:, sl] + y
    prev_scr[0] = a_hi
    prev_scr[1] = a_lo


def _pool_mixer(h3, g, w_pool, scale):
    B, Lp, D = h3.shape
    t = np.arange(BLOCK)
    d_main = t[:, None] - t[None, :]
    d_halo = t[:, None] + BLOCK - t[None, :]
    bm = np.stack([((d_main >= 0) & (d_main < w)) for w in POOL_WINDOWS]).astype(np.float32)
    bh = np.stack([(d_halo < w) for w in POOL_WINDOWS]).astype(np.float32)
    ng = len(POOL_WINDOWS)
    return pl.pallas_call(
        _pool_kernel,
        grid=(B, Lp // BLOCK),
        in_specs=[pl.BlockSpec((None, BLOCK, D), lambda b, j: (b, j, 0)),
                  pl.BlockSpec((1, D), lambda b, j: (0, 0)),
                  pl.BlockSpec((ng, BLOCK, BLOCK), lambda b, j: (0, 0, 0)),
                  pl.BlockSpec((ng, BLOCK, BLOCK), lambda b, j: (0, 0, 0)),
                  pl.BlockSpec((ng, POOL_GD, POOL_GD), lambda b, j: (0, 0, 0)),
                  pl.BlockSpec((1, D), lambda b, j: (0, 0))],
        out_specs=pl.BlockSpec((None, BLOCK, D), lambda b, j: (b, j, 0)),
        out_shape=jax.ShapeDtypeStruct((B, Lp, D), F32),
        scratch_shapes=[pltpu.VMEM((2, BLOCK, D), BF16)],
        input_output_aliases={0: 0},
        compiler_params=_cparams("parallel", "arbitrary"),
        name="pool_mixer",
    )(h3, g.reshape(1, D), jnp.asarray(bm, BF16), jnp.asarray(bh, BF16), w_pool.astype(BF16),
      scale.reshape(1, D))


FFN_CHUNK = 256


def _ffn_kernel(x_ref, g_ref, w1_ref, w3_ref, w2_ref, o_ref, a_scr):
    x = x_ref[...]
    a_scr[...] = _rms(x, g_ref[...]).astype(BF16)
    for c in range(w1_ref.shape[-1] // FFN_CHUNK):
        cs = slice(c * FFN_CHUNK, (c + 1) * FFN_CHUNK)
        a = a_scr[...]
        h1 = jnp.dot(a, w1_ref[:, cs], preferred_element_type=F32)
        h3 = jnp.dot(a, w3_ref[:, cs], preferred_element_type=F32)
        y = jnp.dot((_silu(h1) * h3).astype(BF16), w2_ref[cs, :], preferred_element_type=F32)
        if c == 0:
            o_ref[...] = x + y
        else:
            o_ref[...] += y


def _ffn(x2, g, w1, w3, w2):
    R, D = x2.shape
    F = w1.shape[-1]
    tm = _pick(R, (512, 256, 128))
    return pl.pallas_call(
        _ffn_kernel,
        grid=(R // tm,),
        in_specs=[pl.BlockSpec((tm, D), lambda t: (t, 0)),
                  pl.BlockSpec((1, D), lambda t: (0, 0)),
                  pl.BlockSpec((D, F), lambda t: (0, 0)),
                  pl.BlockSpec((D, F), lambda t: (0, 0)),
                  pl.BlockSpec((F, D), lambda t: (0, 0))],
        out_specs=pl.BlockSpec((tm, D), lambda t: (t, 0)),
        out_shape=jax.ShapeDtypeStruct((R, D), F32),
        scratch_shapes=[pltpu.VMEM((tm, D), BF16)],
        input_output_aliases={0: 0},
        compiler_params=_cparams("parallel"),
        name="swiglu_dense",
    )(x2, g.reshape(1, D), w1, w3, w2)


def _router_kernel(h_ref, g_ref, wr_ref, tri_ref, slab_ref, cnt_ref, carry_scr):
    first = (pl.program_id(0) == 0) & (pl.program_id(1) == 0)

    @pl.when(first)
    def _():
        carry_scr[...] = jnp.zeros_like(carry_scr)

    tr = h_ref.shape[0]
    a = _rms(h_ref[...], g_ref[...])
    a_hi = a.astype(BF16)
    a_lo = (a - a_hi.astype(F32)).astype(BF16)
    wr = wr_ref[...]
    w_hi = wr.astype(BF16)
    w_lo = (wr - w_hi.astype(F32)).astype(BF16)
    logits = (jnp.dot(a_hi, w_hi, preferred_element_type=F32)
              + jnp.dot(a_hi, w_lo, preferred_element_type=F32)
              + jnp.dot(a_lo, w_hi, preferred_element_type=F32))
    lane = lax.broadcasted_iota(jnp.int32, (tr, LANES), 1).astype(F32)
    neg = jnp.float32(-jnp.inf)
    logits = jnp.where(lane < N_EXPERTS, logits, neg)
    m1 = jnp.max(logits, axis=-1, keepdims=True)
    i1 = jnp.min(jnp.where(logits == m1, lane, float(LANES)), axis=-1, keepdims=True)
    rest = jnp.where(lane == i1, neg, logits)
    m2 = jnp.max(rest, axis=-1, keepdims=True)
    i2 = jnp.min(jnp.where(rest == m2, lane, float(LANES)), axis=-1, keepdims=True)
    e = jnp.exp(m2 - m1)
    g1 = 1.0 / (1.0 + e)
    g2 = e / (1.0 + e)
    pos = pl.program_id(1) * tr + lax.broadcasted_iota(jnp.int32, (tr, LANES), 0)
    valid = (pos >= PAD).astype(F32)
    oh1 = (lane == i1).astype(F32)
    oh2 = (lane == i2).astype(F32)
    chosen = (oh1 + oh2) * valid
    before = carry_scr[0:1, :] + jnp.dot(tri_ref[...], chosen.astype(BF16), preferred_element_type=F32)
    r1 = jnp.sum(before * oh1, axis=-1, keepdims=True)
    r2 = jnp.sum(before * oh2, axis=-1, keepdims=True)
    total = carry_scr[0:1, :] + jnp.sum(chosen, axis=0, keepdims=True)
    carry_scr[...] = jnp.broadcast_to(total, carry_scr.shape)
    cnt_ref[...] = jnp.broadcast_to(total, cnt_ref.shape)
    cols = (i1, i2, g1 * valid, g2 * valid, r1, r2)
    slab = jnp.zeros((tr, LANES), F32)
    for c, val in enumerate(cols):
        slab = jnp.where(lane == c, val, slab)
    slab_ref[...] = slab


def _router(h3, g, w_router):
    B, Lp, D = h3.shape
    tr = Lp // 2
    wr = jnp.zeros((D, LANES), F32).at[:, :N_EXPERTS].set(w_router)
    t = np.arange(tr)
    tri = (t[:, None] > t[None, :]).astype(np.float32)
    slab, cnt = pl.pallas_call(
        _router_kernel,
        grid=(B, Lp // tr),
        in_specs=[pl.BlockSpec((None, tr, D), lambda b, j: (b, j, 0)),
                  pl.BlockSpec((1, D), lambda b, j: (0, 0)),
                  pl.BlockSpec((D, LANES), lambda b, j: (0, 0)),
                  pl.BlockSpec((tr, tr), lambda b, j: (0, 0))],
        out_specs=[pl.BlockSpec((tr, LANES), lambda b, j: (b * (Lp // tr) + j, 0)),
                   pl.BlockSpec((8, LANES), lambda b, j: (0, 0))],
        out_shape=[jax.ShapeDtypeStruct((B * Lp, LANES), F32), jax.ShapeDtypeStruct((8, LANES), F32)],
        scratch_shapes=[pltpu.VMEM((8, LANES), F32)],
        compiler_params=_cparams("arbitrary", "arbitrary"),
        name="moe_router",
    )(h3, g.reshape(1, D), wr, jnp.asarray(tri, BF16))
    return slab, cnt[0, :N_EXPERTS]


def _invert_kernel(lo_ref, hi_ref, spare_ref, p1_hbm, p2_hbm, inv_ref, buf1, buf2, sem, *, n_tok):
    chunk = buf1.shape[0]
    tile_mask = spare_ref.shape[0] - 1

    def fill(r, carry):
        inv_ref[r] = spare_ref[r & tile_mask]
        return carry

    for e in range(N_EXPERTS + 1):
        lax.fori_loop(lo_ref[e], hi_ref[e], fill, 0)
    for c in range(n_tok // chunk):
        c1 = pltpu.make_async_copy(p1_hbm.at[pl.ds(c * chunk, chunk)], buf1, sem.at[0])
        c2 = pltpu.make_async_copy(p2_hbm.at[pl.ds(c * chunk, chunk)], buf2, sem.at[1])
        c1.start()
        c2.start()
        c1.wait()
        c2.wait()

        def token(i, carry, c=c):
            inv_ref[buf1[i]] = c * chunk + i
            inv_ref[buf2[i]] = n_tok + c * chunk + i
            return carry

        lax.fori_loop(0, chunk, token, 0, unroll=8)


def _invert(p1, p2, pad_lo, pad_hi, spare, n_rows):
    n_tok = p1.shape[0]
    chunk = _pick(n_tok, (2048, 1024, 512, 256, 128))
    grid_spec = pltpu.PrefetchScalarGridSpec(
        num_scalar_prefetch=3,
        grid=(1,),
        in_specs=[pl.BlockSpec(memory_space=pl.ANY), pl.BlockSpec(memory_space=pl.ANY)],
        out_specs=pl.BlockSpec(memory_space=pltpu.SMEM),
        scratch_shapes=[pltpu.SMEM((chunk,), jnp.int32), pltpu.SMEM((chunk,), jnp.int32),
                        pltpu.SemaphoreType.DMA((2,))])
    return pl.pallas_call(
        functools.partial(_invert_kernel, n_tok=n_tok),
        grid_spec=grid_spec,
        out_shape=jax.ShapeDtypeStruct((n_rows + 1,), jnp.int32),
        compiler_params=_cparams("arbitrary"),
        name="moe_invert",
    )(pad_lo, pad_hi, spare, p1, p2)


MOE_TILE = 512


def _moe_ffn_kernel(te_ref, inv_ref, spare_ref, h_hbm, g_ref, w1_ref, w3_ref, w2_ref, y_hbm, xbuf, ybuf, a_scr,
                    gsem, ssem, *, n_tok, n_batch, seq_pad):
    del te_ref
    t = pl.program_id(0)
    n = pl.num_programs(0)
    tm = xbuf.shape[0]
    slot = t % 2
    other = 1 - slot

    def gather_row(tile, r):
        idx = inv_ref[tile * tm + r]
        src = jnp.where(idx >= n_tok, idx - n_tok, idx)
        pltpu.make_async_copy(h_hbm.at[pl.ds(src, 1)], xbuf.at[pl.ds(r, 1)], gsem.at[0]).start()

    def scatter_row(tile, buf_slot, r):
        dst = jnp.where(tile >= 0, inv_ref[jnp.maximum(tile, 0) * tm + r], spare_ref[r])
        pltpu.make_async_copy(ybuf.at[buf_slot, pl.ds(r, 1)], y_hbm.at[pl.ds(dst, 1)], ssem.at[buf_slot]).start()

    def wait_gather():
        pltpu.make_async_copy(h_hbm.at[pl.ds(0, tm)], xbuf, gsem.at[0]).wait()

    def wait_scatter(buf_slot):
        pltpu.make_async_copy(ybuf.at[buf_slot], y_hbm.at[pl.ds(0, tm)], ssem.at[buf_slot]).wait()

    @pl.when(t == 0)
    def _():
        ybuf[...] = jnp.zeros_like(ybuf)
        zero_copies = [pltpu.make_async_copy(ybuf.at[0, pl.ds(0, PAD)],
                                             y_hbm.at[pl.ds(k * n_tok + b * seq_pad, PAD)], ssem.at[0])
                       for k in range(2) for b in range(n_batch)]
        for cp in zero_copies:
            cp.start()
        for cp in zero_copies:
            cp.wait()

        def first(r, carry):
            gather_row(0, r)
            return carry

        lax.fori_loop(0, tm, first, 0, unroll=8)

    @pl.when(t > 0)
    def _():
        wait_scatter(slot)

    wait_gather()
    a_scr[...] = _rms(xbuf[...], g_ref[...]).astype(BF16)
    nxt = jnp.minimum(t + 1, n - 1)
    prv = t - 1
    n_chunks = w1_ref.shape[-1] // FFN_CHUNK
    bounds = [(c * tm) // n_chunks for c in range(n_chunks + 1)]
    for c in range(n_chunks):
        cs = slice(c * FFN_CHUNK, (c + 1) * FFN_CHUNK)
        a = a_scr[...]
        h1 = jnp.dot(a, w1_ref[0, :, cs], preferred_element_type=F32)
        h3 = jnp.dot(a, w3_ref[0, :, cs], preferred_element_type=F32)
        y = jnp.dot((_silu(h1) * h3).astype(BF16), w2_ref[0, cs, :], preferred_element_type=F32)
        if c == 0:
            ybuf[slot] = y
        else:
            ybuf[slot] += y
        for r in range(bounds[c], bounds[c + 1]):
            gather_row(nxt, r)
            scatter_row(prv, other, r)

    @pl.when(t == n - 1)
    def _():
        def last(r, carry):
            scatter_row(t, slot, r)
            return carry

        wait_scatter(other)
        lax.fori_loop(0, tm, last, 0, unroll=8)
        wait_gather()
        wait_scatter(slot)


def _moe_ffn(h2, g, w1, w3, w2, tile_expert, inv, spare, n_rows, n_batch):
    R, D = h2.shape
    F = w1.shape[-1]
    tm = MOE_TILE
    kern = functools.partial(_moe_ffn_kernel, n_tok=R, n_batch=n_batch, seq_pad=R // n_batch)
    grid_spec = pltpu.PrefetchScalarGridSpec(
        num_scalar_prefetch=3,
        grid=(n_rows // tm,),
        in_specs=[pl.BlockSpec(memory_space=pl.ANY),
                  pl.BlockSpec((1, D), lambda t, te, inv, sp: (0, 0)),
                  pl.BlockSpec((1, D, F), lambda t, te, inv, sp: (te[t], 0, 0)),
                  pl.BlockSpec((1, D, F), lambda t, te, inv, sp: (te[t], 0, 0)),
                  pl.BlockSpec((1, F, D), lambda t, te, inv, sp: (te[t], 0, 0))],
        out_specs=pl.BlockSpec(memory_space=pl.ANY),
        scratch_shapes=[pltpu.VMEM((tm, D), F32), pltpu.VMEM((2, tm, D), F32), pltpu.VMEM((tm, D), BF16),
                        pltpu.SemaphoreType.DMA((1,)), pltpu.SemaphoreType.DMA((2,))])
    return pl.pallas_call(
        kern,
        grid_spec=grid_spec,
        out_shape=jax.ShapeDtypeStruct((2 * R, D), F32),
        compiler_params=_cparams("arbitrary"),
        name="swiglu_experts",
    )(tile_expert, inv, spare, h2, g.reshape(1, D), w1, w3, w2)


def _combine_kernel(h_ref, slab_ref, y1_ref, y2_ref, o_ref):
    slab = slab_ref[...]
    o_ref[...] = h_ref[...] + (slab[:, 2:3] * y1_ref[...] + slab[:, 3:4] * y2_ref[...])


def _combine(h2, slab, y):
    R, D = h2.shape
    tc = _pick(R, (512, 256, 128))
    return pl.pallas_call(
        _combine_kernel,
        grid=(R // tc,),
        in_specs=[pl.BlockSpec((tc, D), lambda i: (i, 0)),
                  pl.BlockSpec((tc, LANES), lambda i: (i, 0)),
                  pl.BlockSpec((tc, D), lambda i: (i, 0)),
                  pl.BlockSpec((tc, D), lambda i: (i + R // tc, 0))],
        out_specs=pl.BlockSpec((tc, D), lambda i: (i, 0)),
        out_shape=jax.ShapeDtypeStruct((R, D), F32),
        input_output_aliases={0: 0},
        compiler_params=_cparams("parallel"),
        name="moe_combine",
    )(h2, slab, y, y)


def _moe(h3, g, w_router, w1, w3, w2):
    B, Lp, D = h3.shape
    tm = MOE_TILE
    n_tokens = B * (Lp - PAD)
    n_tiles = (2 * n_tokens) // tm + N_EXPERTS
    n_rows = n_tiles * tm
    slab, counts = _router(h3, g, w_router)
    counts = counts.astype(jnp.int32)
    tiles_e = (counts + tm - 1) // tm
    end_tile = jnp.cumsum(tiles_e)
    offset = ((end_tile - tiles_e) * tm).astype(F32)
    n_used = end_tile[-1:]
    expert_ids = jnp.arange(N_EXPERTS, dtype=F32)
    e1, e2 = slab[:, 0:1], slab[:, 1:2]
    valid = (jnp.arange(B * Lp) % Lp) >= PAD
    off1 = jnp.sum(jnp.where(e1 == expert_ids[None, :], offset[None, :], 0.0), axis=-1)
    off2 = jnp.sum(jnp.where(e2 == expert_ids[None, :], offset[None, :], 0.0), axis=-1)
    p1 = jnp.where(valid, off1 + slab[:, 4], float(n_rows)).astype(jnp.int32)
    p2 = jnp.where(valid, off2 + slab[:, 5], float(n_rows)).astype(jnp.int32)
    tile_ids = jnp.arange(n_tiles, dtype=jnp.int32)
    last_tile = jnp.maximum(n_used - 1, 0)
    tile_expert = jnp.sum(jnp.minimum(tile_ids, last_tile)[:, None] >= end_tile[None, :], axis=-1).astype(jnp.int32)
    tile_expert = jnp.minimum(tile_expert, N_EXPERTS - 1)
    group_start = (end_tile - tiles_e) * tm
    pad_lo = jnp.concatenate([group_start + counts, n_used * tm]).astype(jnp.int32)
    pad_hi = jnp.concatenate([end_tile * tm, jnp.full((1,), n_rows + 1, jnp.int32)]).astype(jnp.int32)
    spare_rows = [k * B * Lp + b * Lp + o for k in range(2) for b in range(B) for o in range(PAD)]
    assert tm & (tm - 1) == 0 and len(spare_rows) >= tm
    spare = jnp.asarray(np.array(spare_rows[:tm], np.int32))
    inv = _invert(p1, p2, pad_lo, pad_hi, spare, n_rows)
    h2 = h3.reshape(B * Lp, D)
    ys = _moe_ffn(h2, g, w1, w3, w2, tile_expert, inv, spare, n_rows, B)
    return _combine(h2, slab, ys).reshape(B, Lp, D)


def _final_norm_kernel(h_ref, g_ref, o_ref):
    o_ref[...] = _rms(h_ref[...], g_ref[...])


def _final_norm(h3, g):
    B, Lp, D = h3.shape
    lead = (PAD + N_META) // BLOCK
    n = Lp // BLOCK - lead
    return pl.pallas_call(
        _final_norm_kernel,
        grid=(B, n),
        in_specs=[pl.BlockSpec((None, BLOCK, D), lambda b, j: (b, j + lead, 0)),
                  pl.BlockSpec((1, D), lambda b, j: (0, 0))],
        out_specs=pl.BlockSpec((None, BLOCK, D), lambda b, j: (b, j, 0)),
        out_shape=jax.ShapeDtypeStruct((B, n * BLOCK, D), F32),
        compiler_params=_cparams("parallel", "parallel"),
        name="final_norm",
    )(h3, g.reshape(1, D))


def kernel(x, meta_tokens, norm_mix_g, norm_ffn_g, ret_wq, ret_wk, ret_wv, ret_wg, ret_wo, ret_gn_g,
           sb_wqkv, sb_wo, pool_w, pool_scale, ffn_w1, ffn_w3, ffn_w2,
           moe_router, moe_w1, moe_w3, moe_w2, final_norm_g):
    B, S, D = x.shape
    Lp = PAD + N_META + S
    R = B * Lp
    depth = norm_mix_g.shape[0]
    meta = jnp.broadcast_to(meta_tokens.astype(x.dtype)[None], (B, N_META, D))
    h = jnp.concatenate([jnp.zeros((B, PAD, D), x.dtype), meta, x], axis=1)
    for i in range(depth):
        m, j = i % 3, i // 3
        if m == 0:
            w_in = jnp.concatenate([ret_wq[j], ret_wk[j], ret_wv[j], ret_wg[j]], axis=1).astype(BF16)
            p = _norm_matmul(h.reshape(R, D), norm_mix_g[i], w_in)
            y = _retention(p.reshape(B, Lp, -1), ret_gn_g[j])
            h = _matmul_residual(y.reshape(R, -1), ret_wo[j].astype(BF16), h.reshape(R, D)).reshape(B, Lp, D)
        elif m == 1:
            qkv = _norm_matmul(h.reshape(R, D), norm_mix_g[i], sb_wqkv[j].astype(BF16))
            y = _stick_breaking(qkv.reshape(B, Lp, -1))
            h = _matmul_residual(y.reshape(R, D), sb_wo[j].astype(BF16), h.reshape(R, D)).reshape(B, Lp, D)
        else:
            h = _pool_mixer(h, norm_mix_g[i], pool_w[j], pool_scale[j])
        c = i // 2
        if i % 2 == 0:
            h = _ffn(h.reshape(R, D), norm_ffn_g[i], ffn_w1[c].astype(BF16), ffn_w3[c].astype(BF16),
                     ffn_w2[c].astype(BF16)).reshape(B, Lp, D)
        else:
            h = _moe(h, norm_ffn_g[i], moe_router[c], moe_w1[c].astype(BF16), moe_w3[c].astype(BF16),
                     moe_w2[c].astype(BF16))
    return _final_norm(h, final_norm_g)
```

```python
import functools

import numpy as np
import jax
import jax.numpy as jnp
from jax import lax
from jax.experimental import pallas as pl
from jax.experimental.pallas import tpu as pltpu

F32 = jnp.float32
BF16 = jnp.bfloat16

D_MODEL = 1024
N_META = 16
BLOCK = 128
PAD = BLOCK - N_META
EPS = 1e-6
RET_HEADS = 8
RET_DK = D_MODEL // RET_HEADS
RET_DV = 2 * D_MODEL // RET_HEADS
ROPE_BASE = 10000.0
SB_HEADS = 16
SB_HD = D_MODEL // SB_HEADS
POOL_WINDOWS = (2, 4, 8, 16)
POOL_GD = D_MODEL // len(POOL_WINDOWS)
N_EXPERTS = 8
LANES = 128
VMEM_LIMIT_BYTES = 58 * 1024 * 1024


def _cparams(*sem):
    return pltpu.CompilerParams(dimension_semantics=sem, vmem_limit_bytes=VMEM_LIMIT_BYTES)


def _pick(n, cands):
    for c in cands:
        if n % c == 0:
            return c
    raise ValueError(f"no tile for {n} in {cands}")


def _rms(x, g):
    ms = jnp.mean(x * x, axis=-1, keepdims=True)
    return x * lax.rsqrt(ms + EPS) * g


def _silu(x):
    return x * (1.0 / (1.0 + jnp.exp(-x)))


def _norm_matmul_kernel(h_ref, g_ref, w_ref, o_ref, a_scr):
    @pl.when(pl.program_id(1) == 0)
    def _():
        a_scr[...] = _rms(h_ref[...], g_ref[...]).astype(BF16)

    o_ref[...] = jnp.dot(a_scr[...], w_ref[...], preferred_element_type=F32).astype(o_ref.dtype)


def _norm_matmul(h2, g, w):
    R, D = h2.shape
    N = w.shape[1]
    tm = _pick(R, (1024, 512, 256, 128))
    tn = _pick(N, (1024, 512, 256, 128))
    return pl.pallas_call(
        _norm_matmul_kernel,
        grid=(R // tm, N // tn),
        in_specs=[pl.BlockSpec((tm, D), lambda i, j: (i, 0)),
                  pl.BlockSpec((1, D), lambda i, j: (0, 0)),
                  pl.BlockSpec((D, tn), lambda i, j: (0, j))],
        out_specs=pl.BlockSpec((tm, tn), lambda i, j: (i, j)),
        out_shape=jax.ShapeDtypeStruct((R, N), BF16),
        scratch_shapes=[pltpu.VMEM((tm, D), BF16)],
        compiler_params=_cparams("parallel", "arbitrary"),
        name="norm_matmul",
    )(h2, g.reshape(1, D), w)


def _matmul_residual_kernel(y_ref, w_ref, h_ref, o_ref):
    o_ref[...] = h_ref[...] + jnp.dot(y_ref[...], w_ref[...], preferred_element_type=F32)


def _matmul_residual(y, w, h2):
    R, K = y.shape
    D = w.shape[1]
    tm = _pick(R, (512, 256, 128))
    return pl.pallas_call(
        _matmul_residual_kernel,
        grid=(R // tm,),
        in_specs=[pl.BlockSpec((tm, K), lambda i: (i, 0)),
                  pl.BlockSpec((K, D), lambda i: (0, 0)),
                  pl.BlockSpec((tm, D), lambda i: (i, 0))],
        out_specs=pl.BlockSpec((tm, D), lambda i: (i, 0)),
        out_shape=jax.ShapeDtypeStruct((R, D), F32),
        input_output_aliases={2: 0},
        compiler_params=_cparams("parallel"),
        name="matmul_residual",
    )(y, w, h2)


def _retention_kernel(p_ref, cos_ref, sin_ref, dec_ref, zeta_ref, xi_ref, cd_ref, gn_ref, o_ref, state_scr):
    @pl.when(pl.program_id(1) == 0)
    def _():
        state_scr[...] = jnp.zeros_like(state_scr)

    cos = cos_ref[...]
    sin = sin_ref[...]
    k0, v0, g0 = D_MODEL, 2 * D_MODEL, 4 * D_MODEL
    for h in range(RET_HEADS):
        q = p_ref[:, h * RET_DK:(h + 1) * RET_DK].astype(F32)
        k = p_ref[:, k0 + h * RET_DK:k0 + (h + 1) * RET_DK].astype(F32)
        v = p_ref[:, v0 + h * RET_DV:v0 + (h + 1) * RET_DV]
        gate = p_ref[:, g0 + h * RET_DV:g0 + (h + 1) * RET_DV].astype(F32)
        qr = q * cos + pltpu.roll(q, RET_DK // 2, 1) * sin
        kr = (k * cos + pltpu.roll(k, RET_DK // 2, 1) * sin) * (RET_DK ** -0.5)
        qb = qr.astype(BF16)
        kb = kr.astype(BF16)
        s = lax.dot_general(qb, kb, (((1,), (1,)), ((), ())), preferred_element_type=F32) * dec_ref[h]
        intra = jnp.dot(s.astype(BF16), v, preferred_element_type=F32)
        st = state_scr[h]
        cross = jnp.dot(qb, st.astype(BF16), preferred_element_type=F32) * xi_ref[h]
        y = intra + cross
        kz = (kr * zeta_ref[h]).astype(BF16)
        kv = lax.dot_general(kz, v, (((0,), (0,)), ((), ())), preferred_element_type=F32)
        state_scr[h] = st * cd_ref[h] + kv
        mu = jnp.mean(y, axis=-1, keepdims=True)
        d = y - mu
        var = jnp.mean(d * d, axis=-1, keepdims=True)
        yn = d * lax.rsqrt(var + EPS) * gn_ref[:, h * RET_DV:(h + 1) * RET_DV]
        o_ref[:, h * RET_DV:(h + 1) * RET_DV] = (_silu(gate) * yn).astype(o_ref.dtype)


def _retention_tables(Lp):
    half = RET_DK // 2
    inv = ROPE_BASE ** (-jnp.arange(half, dtype=F32) / half)
    pos = (jnp.arange(Lp) - PAD).astype(F32)
    ang = pos[:, None] * inv[None, :]
    cos, sin = jnp.cos(ang), jnp.sin(ang)
    cos2 = jnp.concatenate([cos, cos], axis=-1)
    sin2 = jnp.concatenate([-sin, sin], axis=-1)
    log_g = jnp.log1p(-jnp.exp2(-5.0 - jnp.arange(RET_HEADS, dtype=F32)))
    idx = jnp.arange(BLOCK, dtype=F32)
    diff = idx[:, None] - idx[None, :]
    decay = jnp.where(diff >= 0, jnp.exp(jnp.maximum(diff, 0.0)[None] * log_g[:, None, None]), 0.0)
    zeta = jnp.exp((BLOCK - 1 - idx)[None, :] * log_g[:, None])
    xi = jnp.exp((idx + 1)[None, :] * log_g[:, None])
    cd = jnp.exp(BLOCK * log_g)
    zeta_b = jnp.broadcast_to(zeta[:, :, None], (RET_HEADS, BLOCK, RET_DK))
    xi_b = jnp.broadcast_to(xi[:, :, None], (RET_HEADS, BLOCK, RET_DV))
    cd_b = jnp.broadcast_to(cd[:, None, None], (RET_HEADS, 1, RET_DV))
    return cos2, sin2, decay, zeta_b, xi_b, cd_b


def _retention(p3, gn_g):
    B, Lp, W = p3.shape
    cos2, sin2, decay, zeta_b, xi_b, cd_b = _retention_tables(Lp)
    const3 = lambda b, c: (0, 0, 0)
    return pl.pallas_call(
        _retention_kernel,
        grid=(B, Lp // BLOCK),
        in_specs=[pl.BlockSpec((None, BLOCK, W), lambda b, c: (b, c, 0)),
                  pl.BlockSpec((BLOCK, RET_DK), lambda b, c: (c, 0)),
                  pl.BlockSpec((BLOCK, RET_DK), lambda b, c: (c, 0)),
                  pl.BlockSpec((RET_HEADS, BLOCK, BLOCK), const3),
                  pl.BlockSpec((RET_HEADS, BLOCK, RET_DK), const3),
                  pl.BlockSpec((RET_HEADS, BLOCK, RET_DV), const3),
                  pl.BlockSpec((RET_HEADS, 1, RET_DV), const3),
                  pl.BlockSpec((1, 2 * D_MODEL), lambda b, c: (0, 0))],
        out_specs=pl.BlockSpec((None, BLOCK, 2 * D_MODEL), lambda b, c: (b, c, 0)),
        out_shape=jax.ShapeDtypeStruct((B, Lp, 2 * D_MODEL), BF16),
        scratch_shapes=[pltpu.VMEM((RET_HEADS, RET_DK, RET_DV), F32)],
        compiler_params=_cparams("parallel", "arbitrary"),
        name="retention",
    )(p3, cos2, sin2, decay, zeta_b, xi_b, cd_b, gn_g.reshape(1, 2 * D_MODEL))


SB_PAIRS = D_MODEL // LANES


def _sb_kernel(q_ref, k_ref, v_ref, u_ref, o_ref, q_scr, acc_scr, car_scr):
    i = pl.program_id(1)
    lane = lax.broadcasted_iota(jnp.int32, (BLOCK, LANES), 1)
    for p in range(SB_PAIRS):
        qs = q_ref[:, p * LANES:(p + 1) * LANES] * (SB_HD ** -0.5)
        zero = jnp.zeros_like(qs)
        q_scr[p, :BLOCK] = jnp.where(lane < SB_HD, qs, zero)
        q_scr[p, BLOCK:] = jnp.where(lane >= SB_HD, qs, zero)
    row = lax.broadcasted_iota(jnp.int32, (2 * BLOCK, BLOCK), 0)
    col = lax.broadcasted_iota(jnp.int32, (2 * BLOCK, BLOCK), 1)
    qpos = i * BLOCK + (row & (BLOCK - 1))
    acc_scr[...] = jnp.zeros_like(acc_scr)
    car_scr[...] = jnp.zeros_like(car_scr)

    def key_block(j, masked):
        rows = pl.ds(pl.multiple_of(j * BLOCK, BLOCK), BLOCK)
        if masked:
            kpos = j * BLOCK + col
            mask = (kpos < qpos) & (kpos >= PAD)
            sel = lambda x: jnp.where(mask, x, 0.0)
        else:
            sel = lambda x: x
        pairs = range(SB_PAIRS)
        cols = [slice(p * LANES, (p + 1) * LANES) for p in pairs]
        z = [lax.dot_general(q_scr[p], k_ref[rows, cols[p]], (((1,), (1,)), ((), ())),
                             preferred_element_type=F32) for p in pairs]
        log_beta = [jnp.minimum(z[p], 0.0) - jnp.log(1.0 + jnp.exp(-jnp.abs(z[p]))) for p in pairs]
        log_1m = [sel(log_beta[p] - z[p]).astype(BF16) for p in pairs]
        r = [jnp.dot(log_1m[p], u_ref[...], preferred_element_type=F32) for p in pairs]
        w = [sel(jnp.exp(log_beta[p] + (car_scr[p] + r[p][:, :BLOCK]))).astype(BF16) for p in pairs]
        o = [jnp.dot(w[p], v_ref[rows, cols[p]], preferred_element_type=F32) for p in pairs]
        for p in pairs:
            acc_scr[p] += o[p]
            car_scr[p] += r[p][:, BLOCK:]

    key_block(i, True)

    def interior(t, carry):
        key_block(i - 1 - t, False)
        return carry

    lax.fori_loop(0, i - 1, interior, 0)

    @pl.when(i > 0)
    def _():
        key_block(0, True)
    for p in range(SB_PAIRS):
        o_ref[:, p * LANES:(p + 1) * LANES] = jnp.where(
            lane < SB_HD, acc_scr[p, :BLOCK], acc_scr[p, BLOCK:]).astype(o_ref.dtype)


def _stick_breaking(qkv3):
    B, Lp, _ = qkv3.shape
    j_idx = np.arange(BLOCK)
    u = np.concatenate([(j_idx[:, None] > j_idx[None, :]).astype(np.float32),
                        np.ones((BLOCK, BLOCK), np.float32)], axis=1)
    return pl.pallas_call(
        _sb_kernel,
        grid=(B, Lp // BLOCK),
        in_specs=[pl.BlockSpec((None, BLOCK, D_MODEL), lambda b, i: (b, i, 0)),
                  pl.BlockSpec((None, Lp, D_MODEL), lambda b, i: (b, 0, 1)),
                  pl.BlockSpec((None, Lp, D_MODEL), lambda b, i: (b, 0, 2)),
                  pl.BlockSpec((BLOCK, 2 * BLOCK), lambda b, i: (0, 0))],
        out_specs=pl.BlockSpec((None, BLOCK, D_MODEL), lambda b, i: (b, i, 0)),
        out_shape=jax.ShapeDtypeStruct((B, Lp, D_MODEL), BF16),
        scratch_shapes=[pltpu.VMEM((SB_PAIRS, 2 * BLOCK, LANES), BF16),
                        pltpu.VMEM((SB_PAIRS, 2 * BLOCK, LANES), F32),
                        pltpu.VMEM((SB_PAIRS, 2 * BLOCK, BLOCK), F32)],
        compiler_params=_cparams("parallel", "arbitrary"),
        name="stick_breaking",
    )(qkv3, qkv3, qkv3, jnp.asarray(u, BF16))


def _pool_kernel(h_ref, g_ref, bm_ref, bh_ref, w_ref, sc_ref, o_ref, prev_scr):
    j = pl.program_id(1)

    @pl.when(j == 0)
    def _():
        prev_scr[...] = jnp.zeros_like(prev_scr)

    x = h_ref[...]
    pos = j * BLOCK + lax.broadcasted_iota(jnp.int32, (BLOCK, POOL_GD), 0) - PAD
    a = _rms(x, g_ref[...])
    a_hi = a.astype(BF16)
    a_lo = (a - a_hi.astype(F32)).astype(BF16)
    for g, win in enumerate(POOL_WINDOWS):
        sl = slice(g * POOL_GD, (g + 1) * POOL_GD)
        ws = (jnp.dot(bm_ref[g], a_hi[:, sl], preferred_element_type=F32)
              + jnp.dot(bm_ref[g], a_lo[:, sl], preferred_element_type=F32)
              + jnp.dot(bh_ref[g], prev_scr[0, :, sl], preferred_element_type=F32)
              + jnp.dot(bh_ref[g], prev_scr[1, :, sl], preferred_element_type=F32))
        cnt = jnp.clip(pos + 1, 1, win).astype(F32)
        diff = ws / cnt - a[:, sl]
        y = jnp.dot(diff.astype(BF16), w_ref[g], preferred_element_type=F32) * sc_ref[:, sl]
        o_ref[:, sl] = x[:, sl] + y
    prev_scr[0] = a_hi
    prev_scr[1] = a_lo


def _pool_mixer(h3, g, w_pool, scale):
    B, Lp, D = h3.shape
    t = np.arange(BLOCK)
    d_main = t[:, None] - t[None, :]
    d_halo = t[:, None] + BLOCK - t[None, :]
    bm = np.stack([((d_main >= 0) & (d_main < w)) for w in POOL_WINDOWS]).astype(np.float32)
    bh = np.stack([(d_halo < w) for w in POOL_WINDOWS]).astype(np.float32)
    ng = len(POOL_WINDOWS)
    return pl.pallas_call(
        _pool_kernel,
        grid=(B, Lp // BLOCK),
        in_specs=[pl.BlockSpec((None, BLOCK, D), lambda b, j: (b, j, 0)),
                  pl.BlockSpec((1, D), lambda b, j: (0, 0)),
                  pl.BlockSpec((ng, BLOCK, BLOCK), lambda b, j: (0, 0, 0)),
                  pl.BlockSpec((ng, BLOCK, BLOCK), lambda b, j: (0, 0, 0)),
                  pl.BlockSpec((ng, POOL_GD, POOL_GD), lambda b, j: (0, 0, 0)),
                  pl.BlockSpec((1, D), lambda b, j: (0, 0))],
        out_specs=pl.BlockSpec((None, BLOCK, D), lambda b, j: (b, j, 0)),
        out_shape=jax.ShapeDtypeStruct((B, Lp, D), F32),
        scratch_shapes=[pltpu.VMEM((2, BLOCK, D), BF16)],
        input_output_aliases={0: 0},
        compiler_params=_cparams("parallel", "arbitrary"),
        name="pool_mixer",
    )(h3, g.reshape(1, D), jnp.asarray(bm, BF16), jnp.asarray(bh, BF16), w_pool.astype(BF16),
      scale.reshape(1, D))


FFN_CHUNK = 256


def _ffn_kernel(x_ref, g_ref, w1_ref, w3_ref, w2_ref, o_ref, a_scr):
    x = x_ref[...]
    a_scr[...] = _rms(x, g_ref[...]).astype(BF16)
    for c in range(w1_ref.shape[-1] // FFN_CHUNK):
        cs = slice(c * FFN_CHUNK, (c + 1) * FFN_CHUNK)
        a = a_scr[...]
        h1 = jnp.dot(a, w1_ref[:, cs], preferred_element_type=F32)
        h3 = jnp.dot(a, w3_ref[:, cs], preferred_element_type=F32)
        y = jnp.dot((_silu(h1) * h3).astype(BF16), w2_ref[cs, :], preferred_element_type=F32)
        if c == 0:
            o_ref[...] = x + y
        else:
            o_ref[...] += y


def _ffn(x2, g, w1, w3, w2):
    R, D = x2.shape
    F = w1.shape[-1]
    tm = _pick(R, (512, 256, 128))
    return pl.pallas_call(
        _ffn_kernel,
        grid=(R // tm,),
        in_specs=[pl.BlockSpec((tm, D), lambda t: (t, 0)),
                  pl.BlockSpec((1, D), lambda t: (0, 0)),
                  pl.BlockSpec((D, F), lambda t: (0, 0)),
                  pl.BlockSpec((D, F), lambda t: (0, 0)),
                  pl.BlockSpec((F, D), lambda t: (0, 0))],
        out_specs=pl.BlockSpec((tm, D), lambda t: (t, 0)),
        out_shape=jax.ShapeDtypeStruct((R, D), F32),
        scratch_shapes=[pltpu.VMEM((tm, D), BF16)],
        input_output_aliases={0: 0},
        compiler_params=_cparams("parallel"),
        name="swiglu_dense",
    )(x2, g.reshape(1, D), w1, w3, w2)


def _router_kernel(h_ref, g_ref, wr_ref, tri_ref, slab_ref, cnt_ref, carry_scr):
    first = (pl.program_id(0) == 0) & (pl.program_id(1) == 0)

    @pl.when(first)
    def _():
        carry_scr[...] = jnp.zeros_like(carry_scr)

    tr = h_ref.shape[0]
    a = _rms(h_ref[...], g_ref[...])
    a_hi = a.astype(BF16)
    a_lo = (a - a_hi.astype(F32)).astype(BF16)
    wr = wr_ref[...]
    w_hi = wr.astype(BF16)
    w_lo = (wr - w_hi.astype(F32)).astype(BF16)
    logits = (jnp.dot(a_hi, w_hi, preferred_element_type=F32)
              + jnp.dot(a_hi, w_lo, preferred_element_type=F32)
              + jnp.dot(a_lo, w_hi, preferred_element_type=F32))
    lane = lax.broadcasted_iota(jnp.int32, (tr, LANES), 1).astype(F32)
    neg = jnp.float32(-jnp.inf)
    logits = jnp.where(lane < N_EXPERTS, logits, neg)
    m1 = jnp.max(logits, axis=-1, keepdims=True)
    i1 = jnp.min(jnp.where(logits == m1, lane, float(LANES)), axis=-1, keepdims=True)
    rest = jnp.where(lane == i1, neg, logits)
    m2 = jnp.max(rest, axis=-1, keepdims=True)
    i2 = jnp.min(jnp.where(rest == m2, lane, float(LANES)), axis=-1, keepdims=True)
    e = jnp.exp(m2 - m1)
    g1 = 1.0 / (1.0 + e)
    g2 = e / (1.0 + e)
    pos = pl.program_id(1) * tr + lax.broadcasted_iota(jnp.int32, (tr, LANES), 0)
    valid = (pos >= PAD).astype(F32)
    oh1 = (lane == i1).astype(F32)
    oh2 = (lane == i2).astype(F32)
    chosen = (oh1 + oh2) * valid
    before = carry_scr[0:1, :] + jnp.dot(tri_ref[...], chosen.astype(BF16), preferred_element_type=F32)
    r1 = jnp.sum(before * oh1, axis=-1, keepdims=True)
    r2 = jnp.sum(before * oh2, axis=-1, keepdims=True)
    total = carry_scr[0:1, :] + jnp.sum(chosen, axis=0, keepdims=True)
    carry_scr[...] = jnp.broadcast_to(total, carry_scr.shape)
    cnt_ref[...] = jnp.broadcast_to(total, cnt_ref.shape)
    cols = (i1, i2, g1 * valid, g2 * valid, r1, r2)
    slab = jnp.zeros((tr, LANES), F32)
    for c, val in enumerate(cols):
        slab = jnp.where(lane == c, val, slab)
    slab_ref[...] = slab


def _router(h3, g, w_router):
    B, Lp, D = h3.shape
    tr = Lp // 2
    wr = jnp.zeros((D, LANES), F32).at[:, :N_EXPERTS].set(w_router)
    t = np.arange(tr)
    tri = (t[:, None] > t[None, :]).astype(np.float32)
    slab, cnt = pl.pallas_call(
        _router_kernel,
        grid=(B, Lp // tr),
        in_specs=[pl.BlockSpec((None, tr, D), lambda b, j: (b, j, 0)),
                  pl.BlockSpec((1, D), lambda b, j: (0, 0)),
                  pl.BlockSpec((D, LANES), lambda b, j: (0, 0)),
                  pl.BlockSpec((tr, tr), lambda b, j: (0, 0))],
        out_specs=[pl.BlockSpec((tr, LANES), lambda b, j: (b * (Lp // tr) + j, 0)),
                   pl.BlockSpec((8, LANES), lambda b, j: (0, 0))],
        out_shape=[jax.ShapeDtypeStruct((B * Lp, LANES), F32), jax.ShapeDtypeStruct((8, LANES), F32)],
        scratch_shapes=[pltpu.VMEM((8, LANES), F32)],
        compiler_params=_cparams("arbitrary", "arbitrary"),
        name="moe_router",
    )(h3, g.reshape(1, D), wr, jnp.asarray(tri, BF16))
    return slab, cnt[0, :N_EXPERTS]


def _invert_kernel(lo_ref, hi_ref, spare_ref, p1_hbm, p2_hbm, inv_ref, buf1, buf2, sem, *, n_tok):
    chunk = buf1.shape[0]
    tile_mask = spare_ref.shape[0] - 1

    def fill(r, carry):
        inv_ref[r] = spare_ref[r & tile_mask]
        return carry

    for e in range(N_EXPERTS + 1):
        lax.fori_loop(lo_ref[e], hi_ref[e], fill, 0)
    for c in range(n_tok // chunk):
        c1 = pltpu.make_async_copy(p1_hbm.at[pl.ds(c * chunk, chunk)], buf1, sem.at[0])
        c2 = pltpu.make_async_copy(p2_hbm.at[pl.ds(c * chunk, chunk)], buf2, sem.at[1])
        c1.start()
        c2.start()
        c1.wait()
        c2.wait()

        def token(i, carry, c=c):
            inv_ref[buf1[i]] = c * chunk + i
            inv_ref[buf2[i]] = n_tok + c * chunk + i
            return carry

        lax.fori_loop(0, chunk, token, 0, unroll=8)


def _invert(p1, p2, pad_lo, pad_hi, spare, n_rows):
    n_tok = p1.shape[0]
    chunk = _pick(n_tok, (2048, 1024, 512, 256, 128))
    grid_spec = pltpu.PrefetchScalarGridSpec(
        num_scalar_prefetch=3,
        grid=(1,),
        in_specs=[pl.BlockSpec(memory_space=pl.ANY), pl.BlockSpec(memory_space=pl.ANY)],
        out_specs=pl.BlockSpec(memory_space=pltpu.SMEM),
        scratch_shapes=[pltpu.SMEM((chunk,), jnp.int32), pltpu.SMEM((chunk,), jnp.int32),
                        pltpu.SemaphoreType.DMA((2,))])
    return pl.pallas_call(
        functools.partial(_invert_kernel, n_tok=n_tok),
        grid_spec=grid_spec,
        out_shape=jax.ShapeDtypeStruct((n_rows + 1,), jnp.int32),
        compiler_params=_cparams("arbitrary"),
        name="moe_invert",
    )(pad_lo, pad_hi, spare, p1, p2)


MOE_TILE = 512
MOE_GATES = 4


def _moe_ffn_kernel(te_ref, inv_ref, spare_ref, zero_ref, h_hbm, g_ref, w1_ref, w3_ref, w2_ref, y_hbm,
                    xbuf, ybuf, a_scr, gsem, ssem, *, n_tok, n_batch, seq_pad):
    del te_ref
    t = pl.program_id(0)
    n = pl.num_programs(0)
    tm = xbuf.shape[0]
    slot = t % 2
    other = 1 - slot

    def gather_row(tile, r, off=0):
        idx = inv_ref[tile * tm + r + off]
        src = jnp.where(idx >= n_tok, idx - n_tok, idx)
        pltpu.make_async_copy(h_hbm.at[pl.ds(src, 1)], xbuf.at[pl.ds(r, 1)], gsem.at[0]).start()

    def scatter_row(tile, buf_slot, r, off=0):
        dst = jnp.where(tile >= 0, inv_ref[jnp.maximum(tile, 0) * tm + r + off], spare_ref[r])
        pltpu.make_async_copy(ybuf.at[buf_slot, pl.ds(r, 1)], y_hbm.at[pl.ds(dst, 1)], ssem.at[buf_slot]).start()

    def wait_gather():
        pltpu.make_async_copy(h_hbm.at[pl.ds(0, tm)], xbuf, gsem.at[0]).wait()

    def wait_scatter(buf_slot):
        pltpu.make_async_copy(ybuf.at[buf_slot], y_hbm.at[pl.ds(0, tm)], ssem.at[buf_slot]).wait()

    @pl.when(t == 0)
    def _():
        ybuf[...] = jnp.zeros_like(ybuf)
        zero_copies = [pltpu.make_async_copy(ybuf.at[0, pl.ds(0, PAD)],
                                             y_hbm.at[pl.ds(k * n_tok + b * seq_pad, PAD)], ssem.at[0])
                       for k in range(2) for b in range(n_batch)]
        for cp in zero_copies:
            cp.start()
        for cp in zero_copies:
            cp.wait()

        def first(r, carry):
            gather_row(0, r)
            return carry

        lax.fori_loop(0, tm, first, 0, unroll=8)

    @pl.when(t > 0)
    def _():
        wait_scatter(slot)

    wait_gather()
    a_scr[...] = _rms(xbuf[...], g_ref[...]).astype(BF16)
    nxt = jnp.minimum(t + 1, n - 1)
    prv = t - 1
    n_chunks = w1_ref.shape[-1] // FFN_CHUNK
    bounds = [(c * tm) // n_chunks for c in range(n_chunks + 1)]
    for c in range(n_chunks):
        cs = slice(c * FFN_CHUNK, (c + 1) * FFN_CHUNK)
        a = a_scr[...]
        h1 = jnp.dot(a, w1_ref[0, :, cs], preferred_element_type=F32)
        h3 = jnp.dot(a, w3_ref[0, :, cs], preferred_element_type=F32)
        y = jnp.dot((_silu(h1) * h3).astype(BF16), w2_ref[0, cs, :], preferred_element_type=F32)
        if c == 0:
            ybuf[slot] = y
        else:
            ybuf[slot] += y
        gates = [res[rb * (tm // MOE_GATES):rb * (tm // MOE_GATES) + 1, 0:1]
                 for res in (h1, h3, y) for rb in range(MOE_GATES)]
        rows = list(range(bounds[c], bounds[c + 1]))
        for gi, gate in enumerate(gates):
            off = gate[0, 0].astype(jnp.int32) * zero_ref[0]
            for r in rows[(gi * len(rows)) // len(gates):((gi + 1) * len(rows)) // len(gates)]:
                gather_row(nxt, r, off)
                scatter_row(prv, other, r, off)

    @pl.when(t == n - 1)
    def _():
        def last(r, carry):
            scatter_row(t, slot, r)
            return carry

        wait_scatter(other)
        lax.fori_loop(0, tm, last, 0, unroll=8)
        wait_gather()
        wait_scatter(slot)


def _moe_ffn(h2, g, w1, w3, w2, tile_expert, inv, spare, n_rows, n_batch):
    R, D = h2.shape
    F = w1.shape[-1]
    tm = MOE_TILE
    kern = functools.partial(_moe_ffn_kernel, n_tok=R, n_batch=n_batch, seq_pad=R // n_batch)
    grid_spec = pltpu.PrefetchScalarGridSpec(
        num_scalar_prefetch=4,
        grid=(n_rows // tm,),
        in_specs=[pl.BlockSpec(memory_space=pl.ANY),
                  pl.BlockSpec((1, D), lambda t, te, inv, sp, z: (0, 0)),
                  pl.BlockSpec((1, D, F), lambda t, te, inv, sp, z: (te[t], 0, 0)),
                  pl.BlockSpec((1, D, F), lambda t, te, inv, sp, z: (te[t], 0, 0)),
                  pl.BlockSpec((1, F, D), lambda t, te, inv, sp, z: (te[t], 0, 0))],
        out_specs=pl.BlockSpec(memory_space=pl.ANY),
        scratch_shapes=[pltpu.VMEM((tm, D), F32), pltpu.VMEM((2, tm, D), F32), pltpu.VMEM((tm, D), BF16),
                        pltpu.SemaphoreType.DMA((1,)), pltpu.SemaphoreType.DMA((2,))])
    return pl.pallas_call(
        kern,
        grid_spec=grid_spec,
        out_shape=jax.ShapeDtypeStruct((2 * R, D), F32),
        compiler_params=_cparams("arbitrary"),
        name="swiglu_experts",
    )(tile_expert, inv, spare, jnp.zeros((1,), jnp.int32), h2, g.reshape(1, D), w1, w3, w2)


def _combine_kernel(h_ref, slab_ref, y1_ref, y2_ref, o_ref):
    slab = slab_ref[...]
    o_ref[...] = h_ref[...] + (slab[:, 2:3] * y1_ref[...] + slab[:, 3:4] * y2_ref[...])


def _combine(h2, slab, y):
    R, D = h2.shape
    tc = _pick(R, (512, 256, 128))
    return pl.pallas_call(
        _combine_kernel,
        grid=(R // tc,),
        in_specs=[pl.BlockSpec((tc, D), lambda i: (i, 0)),
                  pl.BlockSpec((tc, LANES), lambda i: (i, 0)),
                  pl.BlockSpec((tc, D), lambda i: (i, 0)),
                  pl.BlockSpec((tc, D), lambda i: (i + R // tc, 0))],
        out_specs=pl.BlockSpec((tc, D), lambda i: (i, 0)),
        out_shape=jax.ShapeDtypeStruct((R, D), F32),
        input_output_aliases={0: 0},
        compiler_params=_cparams("parallel"),
        name="moe_combine",
    )(h2, slab, y, y)


def _moe(h3, g, w_router, w1, w3, w2):
    B, Lp, D = h3.shape
    tm = MOE_TILE
    n_tokens = B * (Lp - PAD)
    n_tiles = (2 * n_tokens) // tm + N_EXPERTS
    n_rows = n_tiles * tm
    slab, counts = _router(h3, g, w_router)
    counts = counts.astype(jnp.int32)
    tiles_e = (counts + tm - 1) // tm
    end_tile = jnp.cumsum(tiles_e)
    offset = ((end_tile - tiles_e) * tm).astype(F32)
    n_used = end_tile[-1:]
    expert_ids = jnp.arange(N_EXPERTS, dtype=F32)
    e1, e2 = slab[:, 0:1], slab[:, 1:2]
    valid = (jnp.arange(B * Lp) % Lp) >= PAD
    off1 = jnp.sum(jnp.where(e1 == expert_ids[None, :], offset[None, :], 0.0), axis=-1)
    off2 = jnp.sum(jnp.where(e2 == expert_ids[None, :], offset[None, :], 0.0), axis=-1)
    p1 = jnp.where(valid, off1 + slab[:, 4], float(n_rows)).astype(jnp.int32)
    p2 = jnp.where(valid, off2 + slab[:, 5], float(n_rows)).astype(jnp.int32)
    tile_ids = jnp.arange(n_tiles, dtype=jnp.int32)
    last_tile = jnp.maximum(n_used - 1, 0)
    tile_expert = jnp.sum(jnp.minimum(tile_ids, last_tile)[:, None] >= end_tile[None, :], axis=-1).astype(jnp.int32)
    tile_expert = jnp.minimum(tile_expert, N_EXPERTS - 1)
    group_start = (end_tile - tiles_e) * tm
    pad_lo = jnp.concatenate([group_start + counts, n_used * tm]).astype(jnp.int32)
    pad_hi = jnp.concatenate([end_tile * tm, jnp.full((1,), n_rows + 1, jnp.int32)]).astype(jnp.int32)
    spare_rows = [k * B * Lp + b * Lp + o for k in range(2) for b in range(B) for o in range(PAD)]
    assert tm & (tm - 1) == 0 and len(spare_rows) >= tm
    spare = jnp.asarray(np.array(spare_rows[:tm], np.int32))
    inv = _invert(p1, p2, pad_lo, pad_hi, spare, n_rows)
    h2 = h3.reshape(B * Lp, D)
    ys = _moe_ffn(h2, g, w1, w3, w2, tile_expert, inv, spare, n_rows, B)
    return _combine(h2, slab, ys).reshape(B, Lp, D)


def _final_norm_kernel(h_ref, g_ref, o_ref):
    o_ref[...] = _rms(h_ref[...], g_ref[...])


def _final_norm(h3, g):
    B, Lp, D = h3.shape
    lead = (PAD + N_META) // BLOCK
    n = Lp // BLOCK - lead
    return pl.pallas_call(
        _final_norm_kernel,
        grid=(B, n),
        in_specs=[pl.BlockSpec((None, BLOCK, D), lambda b, j: (b, j + lead, 0)),
                  pl.BlockSpec((1, D), lambda b, j: (0, 0))],
        out_specs=pl.BlockSpec((None, BLOCK, D), lambda b, j: (b, j, 0)),
        out_shape=jax.ShapeDtypeStruct((B, n * BLOCK, D), F32),
        compiler_params=_cparams("parallel", "parallel"),
        name="final_norm",
    )(h3, g.reshape(1, D))


def kernel(x, meta_tokens, norm_mix_g, norm_ffn_g, ret_wq, ret_wk, ret_wv, ret_wg, ret_wo, ret_gn_g,
           sb_wqkv, sb_wo, pool_w, pool_scale, ffn_w1, ffn_w3, ffn_w2,
           moe_router, moe_w1, moe_w3, moe_w2, final_norm_g):
    B, S, D = x.shape
    Lp = PAD + N_META + S
    R = B * Lp
    depth = norm_mix_g.shape[0]
    meta = jnp.broadcast_to(meta_tokens.astype(x.dtype)[None], (B, N_META, D))
    h = jnp.concatenate([jnp.zeros((B, PAD, D), x.dtype), meta, x], axis=1)
    for i in range(depth):
        m, j = i % 3, i // 3
        if m == 0:
            w_in = jnp.concatenate([ret_wq[j], ret_wk[j], ret_wv[j], ret_wg[j]], axis=1).astype(BF16)
            p = _norm_matmul(h.reshape(R, D), norm_mix_g[i], w_in)
            y = _retention(p.reshape(B, Lp, -1), ret_gn_g[j])
            h = _matmul_residual(y.reshape(R, -1), ret_wo[j].astype(BF16), h.reshape(R, D)).reshape(B, Lp, D)
        elif m == 1:
            qkv = _norm_matmul(h.reshape(R, D), norm_mix_g[i], sb_wqkv[j].astype(BF16))
            y = _stick_breaking(qkv.reshape(B, Lp, -1))
            h = _matmul_residual(y.reshape(R, D), sb_wo[j].astype(BF16), h.reshape(R, D)).reshape(B, Lp, D)
        else:
            h = _pool_mixer(h, norm_mix_g[i], pool_w[j], pool_scale[j])
        c = i // 2
        if i % 2 == 0:
            h = _ffn(h.reshape(R, D), norm_ffn_g[i], ffn_w1[c].astype(BF16), ffn_w3[c].astype(BF16),
                     ffn_w2[c].astype(BF16)).reshape(B, Lp, D)
        else:
            h = _moe(h, norm_ffn_g[i], moe_router[c], moe_w1[c].astype(BF16), moe_w3[c].astype(BF16),
                     moe_w2[c].astype(BF16))
    return _final_norm(h, final_norm_g)
```

```python
import functools

import numpy as np
import jax
import jax.numpy as jnp
from jax import lax
from jax.experimental import pallas as pl
from jax.experimental.pallas import tpu as pltpu

F32 = jnp.float32
BF16 = jnp.bfloat16

D_MODEL = 1024
N_META = 16
BLOCK = 128
PAD = BLOCK - N_META
EPS = 1e-6
RET_HEADS = 8
RET_DK = D_MODEL // RET_HEADS
RET_DV = 2 * D_MODEL // RET_HEADS
ROPE_BASE = 10000.0
SB_HEADS = 16
SB_HD = D_MODEL // SB_HEADS
POOL_WINDOWS = (2, 4, 8, 16)
POOL_GD = D_MODEL // len(POOL_WINDOWS)
N_EXPERTS = 8
LANES = 128
VMEM_LIMIT_BYTES = 58 * 1024 * 1024


def _cparams(*sem):
    return pltpu.CompilerParams(dimension_semantics=sem, vmem_limit_bytes=VMEM_LIMIT_BYTES)


def _pick(n, cands):
    for c in cands:
        if n % c == 0:
            return c
    raise ValueError(f"no tile for {n} in {cands}")


def _rms(x, g):
    ms = jnp.mean(x * x, axis=-1, keepdims=True)
    return x * lax.rsqrt(ms + EPS) * g


def _silu(x):
    return x * (1.0 / (1.0 + jnp.exp(-x)))


def _norm_matmul_kernel(h_ref, g_ref, w_ref, o_ref, a_scr):
    @pl.when(pl.program_id(1) == 0)
    def _():
        a_scr[...] = _rms(h_ref[...], g_ref[...]).astype(BF16)

    o_ref[...] = jnp.dot(a_scr[...], w_ref[...], preferred_element_type=F32).astype(o_ref.dtype)


def _norm_matmul(h2, g, w):
    R, D = h2.shape
    N = w.shape[1]
    tm = _pick(R, (1024, 512, 256, 128))
    tn = _pick(N, (1024, 512, 256, 128))
    return pl.pallas_call(
        _norm_matmul_kernel,
        grid=(R // tm, N // tn),
        in_specs=[pl.BlockSpec((tm, D), lambda i, j: (i, 0)),
                  pl.BlockSpec((1, D), lambda i, j: (0, 0)),
                  pl.BlockSpec((D, tn), lambda i, j: (0, j))],
        out_specs=pl.BlockSpec((tm, tn), lambda i, j: (i, j)),
        out_shape=jax.ShapeDtypeStruct((R, N), BF16),
        scratch_shapes=[pltpu.VMEM((tm, D), BF16)],
        compiler_params=_cparams("parallel", "arbitrary"),
        name="norm_matmul",
    )(h2, g.reshape(1, D), w)


def _matmul_residual_kernel(y_ref, w_ref, h_ref, o_ref):
    o_ref[...] = h_ref[...] + jnp.dot(y_ref[...], w_ref[...], preferred_element_type=F32)


def _matmul_residual(y, w, h2):
    R, K = y.shape
    D = w.shape[1]
    tm = _pick(R, (512, 256, 128))
    return pl.pallas_call(
        _matmul_residual_kernel,
        grid=(R // tm,),
        in_specs=[pl.BlockSpec((tm, K), lambda i: (i, 0)),
                  pl.BlockSpec((K, D), lambda i: (0, 0)),
                  pl.BlockSpec((tm, D), lambda i: (i, 0))],
        out_specs=pl.BlockSpec((tm, D), lambda i: (i, 0)),
        out_shape=jax.ShapeDtypeStruct((R, D), F32),
        input_output_aliases={2: 0},
        compiler_params=_cparams("parallel"),
        name="matmul_residual",
    )(y, w, h2)


def _retention_kernel(p_ref, cos_ref, sin_ref, dec_ref, zeta_ref, xi_ref, cd_ref, gn_ref, o_ref, state_scr):
    @pl.when(pl.program_id(1) == 0)
    def _():
        state_scr[...] = jnp.zeros_like(state_scr)

    cos = cos_ref[...]
    sin = sin_ref[...]
    k0, v0, g0 = D_MODEL, 2 * D_MODEL, 4 * D_MODEL
    for h in range(RET_HEADS):
        q = p_ref[:, h * RET_DK:(h + 1) * RET_DK].astype(F32)
        k = p_ref[:, k0 + h * RET_DK:k0 + (h + 1) * RET_DK].astype(F32)
        v = p_ref[:, v0 + h * RET_DV:v0 + (h + 1) * RET_DV]
        gate = p_ref[:, g0 + h * RET_DV:g0 + (h + 1) * RET_DV].astype(F32)
        qr = q * cos + pltpu.roll(q, RET_DK // 2, 1) * sin
        kr = (k * cos + pltpu.roll(k, RET_DK // 2, 1) * sin) * (RET_DK ** -0.5)
        qb = qr.astype(BF16)
        kb = kr.astype(BF16)
        s = lax.dot_general(qb, kb, (((1,), (1,)), ((), ())), preferred_element_type=F32) * dec_ref[h]
        intra = jnp.dot(s.astype(BF16), v, preferred_element_type=F32)
        st = state_scr[h]
        cross = jnp.dot(qb, st.astype(BF16), preferred_element_type=F32) * xi_ref[h]
        y = intra + cross
        kz = (kr * zeta_ref[h]).astype(BF16)
        kv = lax.dot_general(kz, v, (((0,), (0,)), ((), ())), preferred_element_type=F32)
        state_scr[h] = st * cd_ref[h] + kv
        mu = jnp.mean(y, axis=-1, keepdims=True)
        d = y - mu
        var = jnp.mean(d * d, axis=-1, keepdims=True)
        yn = d * lax.rsqrt(var + EPS) * gn_ref[:, h * RET_DV:(h + 1) * RET_DV]
        o_ref[:, h * RET_DV:(h + 1) * RET_DV] = (_silu(gate) * yn).astype(o_ref.dtype)


def _retention_tables(Lp):
    half = RET_DK // 2
    inv = ROPE_BASE ** (-jnp.arange(half, dtype=F32) / half)
    pos = (jnp.arange(Lp) - PAD).astype(F32)
    ang = pos[:, None] * inv[None, :]
    cos, sin = jnp.cos(ang), jnp.sin(ang)
    cos2 = jnp.concatenate([cos, cos], axis=-1)
    sin2 = jnp.concatenate([-sin, sin], axis=-1)
    log_g = jnp.log1p(-jnp.exp2(-5.0 - jnp.arange(RET_HEADS, dtype=F32)))
    idx = jnp.arange(BLOCK, dtype=F32)
    diff = idx[:, None] - idx[None, :]
    decay = jnp.where(diff >= 0, jnp.exp(jnp.maximum(diff, 0.0)[None] * log_g[:, None, None]), 0.0)
    zeta = jnp.exp((BLOCK - 1 - idx)[None, :] * log_g[:, None])
    xi = jnp.exp((idx + 1)[None, :] * log_g[:, None])
    cd = jnp.exp(BLOCK * log_g)
    zeta_b = jnp.broadcast_to(zeta[:, :, None], (RET_HEADS, BLOCK, RET_DK))
    xi_b = jnp.broadcast_to(xi[:, :, None], (RET_HEADS, BLOCK, RET_DV))
    cd_b = jnp.broadcast_to(cd[:, None, None], (RET_HEADS, 1, RET_DV))
    return cos2, sin2, decay, zeta_b, xi_b, cd_b


def _retention(p3, gn_g):
    B, Lp, W = p3.shape
    cos2, sin2, decay, zeta_b, xi_b, cd_b = _retention_tables(Lp)
    const3 = lambda b, c: (0, 0, 0)
    return pl.pallas_call(
        _retention_kernel,
        grid=(B, Lp // BLOCK),
        in_specs=[pl.BlockSpec((None, BLOCK, W), lambda b, c: (b, c, 0)),
                  pl.BlockSpec((BLOCK, RET_DK), lambda b, c: (c, 0)),
                  pl.BlockSpec((BLOCK, RET_DK), lambda b, c: (c, 0)),
                  pl.BlockSpec((RET_HEADS, BLOCK, BLOCK), const3),
                  pl.BlockSpec((RET_HEADS, BLOCK, RET_DK), const3),
                  pl.BlockSpec((RET_HEADS, BLOCK, RET_DV), const3),
                  pl.BlockSpec((RET_HEADS, 1, RET_DV), const3),
                  pl.BlockSpec((1, 2 * D_MODEL), lambda b, c: (0, 0))],
        out_specs=pl.BlockSpec((None, BLOCK, 2 * D_MODEL), lambda b, c: (b, c, 0)),
        out_shape=jax.ShapeDtypeStruct((B, Lp, 2 * D_MODEL), BF16),
        scratch_shapes=[pltpu.VMEM((RET_HEADS, RET_DK, RET_DV), F32)],
        compiler_params=_cparams("parallel", "arbitrary"),
        name="retention",
    )(p3, cos2, sin2, decay, zeta_b, xi_b, cd_b, gn_g.reshape(1, 2 * D_MODEL))


SB_PAIRS = D_MODEL // LANES


def _sb_kernel(q_ref, k_ref, v_ref, u_ref, o_ref, q_scr, acc_scr, car_scr):
    i = pl.program_id(1)
    lane = lax.broadcasted_iota(jnp.int32, (BLOCK, LANES), 1)
    for p in range(SB_PAIRS):
        qs = q_ref[:, p * LANES:(p + 1) * LANES] * (SB_HD ** -0.5)
        zero = jnp.zeros_like(qs)
        q_scr[p, :BLOCK] = jnp.where(lane < SB_HD, qs, zero)
        q_scr[p, BLOCK:] = jnp.where(lane >= SB_HD, qs, zero)
    row = lax.broadcasted_iota(jnp.int32, (2 * BLOCK, BLOCK), 0)
    col = lax.broadcasted_iota(jnp.int32, (2 * BLOCK, BLOCK), 1)
    qpos = i * BLOCK + (row & (BLOCK - 1))
    acc_scr[...] = jnp.zeros_like(acc_scr)
    car_scr[...] = jnp.zeros_like(car_scr)

    def key_blocks(js, masked):
        rows = [pl.ds(pl.multiple_of(j * BLOCK, BLOCK), BLOCK) for j in js]
        if masked:
            masks = [((j * BLOCK + col) < qpos) & ((j * BLOCK + col) >= PAD) for j in js]
            sel = lambda b, x: jnp.where(masks[b], x, 0.0)
        else:
            sel = lambda b, x: x
        items = [(b, p) for b in range(len(js)) for p in range(SB_PAIRS)]
        cols = [slice(p * LANES, (p + 1) * LANES) for p in range(SB_PAIRS)]
        z = {(b, p): lax.dot_general(q_scr[p], k_ref[rows[b], cols[p]], (((1,), (1,)), ((), ())),
                                     preferred_element_type=F32) for b, p in items}
        log_beta = {k: jnp.minimum(z[k], 0.0) - jnp.log(1.0 + jnp.exp(-jnp.abs(z[k]))) for k in items}
        log_1m = {(b, p): sel(b, log_beta[b, p] - z[b, p]).astype(BF16) for b, p in items}
        r = {k: jnp.dot(log_1m[k], u_ref[...], preferred_element_type=F32) for k in items}
        after = {p: car_scr[p] for p in range(SB_PAIRS)}
        w = {}
        for b, p in items:
            w[b, p] = sel(b, jnp.exp(log_beta[b, p] + (after[p] + r[b, p][:, :BLOCK]))).astype(BF16)
            after[p] = after[p] + r[b, p][:, BLOCK:]
        o = {(b, p): jnp.dot(w[b, p], v_ref[rows[b], cols[p]], preferred_element_type=F32) for b, p in items}
        for p in range(SB_PAIRS):
            acc_scr[p] += sum(o[b, p] for b in range(len(js)))
            car_scr[p] = after[p]

    key_blocks([i], True)

    def interior(t, carry):
        j = i - 1 - 2 * t
        key_blocks([j, j - 1], False)
        return carry

    lax.fori_loop(0, (i - 1) // 2, interior, 0)

    @pl.when((i > 1) & ((i - 1) % 2 == 1))
    def _():
        key_blocks([1], False)

    @pl.when(i > 0)
    def _():
        key_blocks([0], True)
    for p in range(SB_PAIRS):
        o_ref[:, p * LANES:(p + 1) * LANES] = jnp.where(
            lane < SB_HD, acc_scr[p, :BLOCK], acc_scr[p, BLOCK:]).astype(o_ref.dtype)


def _stick_breaking(qkv3):
    B, Lp, _ = qkv3.shape
    j_idx = np.arange(BLOCK)
    u = np.concatenate([(j_idx[:, None] > j_idx[None, :]).astype(np.float32),
                        np.ones((BLOCK, BLOCK), np.float32)], axis=1)
    return pl.pallas_call(
        _sb_kernel,
        grid=(B, Lp // BLOCK),
        in_specs=[pl.BlockSpec((None, BLOCK, D_MODEL), lambda b, i: (b, i, 0)),
                  pl.BlockSpec((None, Lp, D_MODEL), lambda b, i: (b, 0, 1)),
                  pl.BlockSpec((None, Lp, D_MODEL), lambda b, i: (b, 0, 2)),
                  pl.BlockSpec((BLOCK, 2 * BLOCK), lambda b, i: (0, 0))],
        out_specs=pl.BlockSpec((None, BLOCK, D_MODEL), lambda b, i: (b, i, 0)),
        out_shape=jax.ShapeDtypeStruct((B, Lp, D_MODEL), BF16),
        scratch_shapes=[pltpu.VMEM((SB_PAIRS, 2 * BLOCK, LANES), BF16),
                        pltpu.VMEM((SB_PAIRS, 2 * BLOCK, LANES), F32),
                        pltpu.VMEM((SB_PAIRS, 2 * BLOCK, BLOCK), F32)],
        compiler_params=_cparams("parallel", "arbitrary"),
        name="stick_breaking",
    )(qkv3, qkv3, qkv3, jnp.asarray(u, BF16))


def _pool_kernel(h_ref, g_ref, bm_ref, bh_ref, w_ref, sc_ref, o_ref, prev_scr):
    prev_scr[...] = jnp.zeros_like(prev_scr)

    def tile(j, carry):
        rows = pl.ds(pl.multiple_of(j * BLOCK, BLOCK), BLOCK)
        x = h_ref[rows, :]
        pos = j * BLOCK + lax.broadcasted_iota(jnp.int32, (BLOCK, POOL_GD), 0) - PAD
        a = _rms(x, g_ref[...])
        a_hi = a.astype(BF16)
        a_lo = (a - a_hi.astype(F32)).astype(BF16)
        for g, win in enumerate(POOL_WINDOWS):
            sl = slice(g * POOL_GD, (g + 1) * POOL_GD)
            ws = (jnp.dot(bm_ref[g], a_hi[:, sl], preferred_element_type=F32)
                  + jnp.dot(bm_ref[g], a_lo[:, sl], preferred_element_type=F32)
                  + jnp.dot(bh_ref[g], prev_scr[0, :, sl], preferred_element_type=F32)
                  + jnp.dot(bh_ref[g], prev_scr[1, :, sl], preferred_element_type=F32))
            cnt = jnp.clip(pos + 1, 1, win).astype(F32)
            diff = ws / cnt - a[:, sl]
            y = jnp.dot(diff.astype(BF16), w_ref[g], preferred_element_type=F32) * sc_ref[:, sl]
            o_ref[rows, sl] = x[:, sl] + y
        prev_scr[0] = a_hi
        prev_scr[1] = a_lo
        return carry

    lax.fori_loop(0, h_ref.shape[0] // BLOCK, tile, 0, unroll=2)


def _pool_mixer(h3, g, w_pool, scale):
    B, Lp, D = h3.shape
    t = np.arange(BLOCK)
    d_main = t[:, None] - t[None, :]
    d_halo = t[:, None] + BLOCK - t[None, :]
    bm = np.stack([((d_main >= 0) & (d_main < w)) for w in POOL_WINDOWS]).astype(np.float32)
    bh = np.stack([(d_halo < w) for w in POOL_WINDOWS]).astype(np.float32)
    ng = len(POOL_WINDOWS)
    return pl.pallas_call(
        _pool_kernel,
        grid=(B,),
        in_specs=[pl.BlockSpec((None, Lp, D), lambda b: (b, 0, 0)),
                  pl.BlockSpec((1, D), lambda b: (0, 0)),
                  pl.BlockSpec((ng, BLOCK, BLOCK), lambda b: (0, 0, 0)),
                  pl.BlockSpec((ng, BLOCK, BLOCK), lambda b: (0, 0, 0)),
                  pl.BlockSpec((ng, POOL_GD, POOL_GD), lambda b: (0, 0, 0)),
                  pl.BlockSpec((1, D), lambda b: (0, 0))],
        out_specs=pl.BlockSpec((None, Lp, D), lambda b: (b, 0, 0)),
        out_shape=jax.ShapeDtypeStruct((B, Lp, D), F32),
        scratch_shapes=[pltpu.VMEM((2, BLOCK, D), BF16)],
        input_output_aliases={0: 0},
        compiler_params=_cparams("parallel"),
        name="pool_mixer",
    )(h3, g.reshape(1, D), jnp.asarray(bm, BF16), jnp.asarray(bh, BF16), w_pool.astype(BF16),
      scale.reshape(1, D))


FFN_CHUNK = 256


def _ffn_kernel(x_ref, g_ref, w1_ref, w3_ref, w2_ref, o_ref, a_scr):
    x = x_ref[...]
    a_scr[...] = _rms(x, g_ref[...]).astype(BF16)
    for c in range(w1_ref.shape[-1] // FFN_CHUNK):
        cs = slice(c * FFN_CHUNK, (c + 1) * FFN_CHUNK)
        a = a_scr[...]
        h1 = jnp.dot(a, w1_ref[:, cs], preferred_element_type=F32)
        h3 = jnp.dot(a, w3_ref[:, cs], preferred_element_type=F32)
        y = jnp.dot((_silu(h1) * h3).astype(BF16), w2_ref[cs, :], preferred_element_type=F32)
        if c == 0:
            o_ref[...] = x + y
        else:
            o_ref[...] += y


def _ffn(x2, g, w1, w3, w2):
    R, D = x2.shape
    F = w1.shape[-1]
    tm = _pick(R, (512, 256, 128))
    return pl.pallas_call(
        _ffn_kernel,
        grid=(R // tm,),
        in_specs=[pl.BlockSpec((tm, D), lambda t: (t, 0)),
                  pl.BlockSpec((1, D), lambda t: (0, 0)),
                  pl.BlockSpec((D, F), lambda t: (0, 0)),
                  pl.BlockSpec((D, F), lambda t: (0, 0)),
                  pl.BlockSpec((F, D), lambda t: (0, 0))],
        out_specs=pl.BlockSpec((tm, D), lambda t: (t, 0)),
        out_shape=jax.ShapeDtypeStruct((R, D), F32),
        scratch_shapes=[pltpu.VMEM((tm, D), BF16)],
        input_output_aliases={0: 0},
        compiler_params=_cparams("parallel"),
        name="swiglu_dense",
    )(x2, g.reshape(1, D), w1, w3, w2)


def _router_kernel(h_ref, g_ref, wr_ref, tri_ref, slab_ref, cnt_ref, carry_scr):
    first = (pl.program_id(0) == 0) & (pl.program_id(1) == 0)

    @pl.when(first)
    def _():
        carry_scr[...] = jnp.zeros_like(carry_scr)

    tr = h_ref.shape[0]
    a = _rms(h_ref[...], g_ref[...])
    a_hi = a.astype(BF16)
    a_lo = (a - a_hi.astype(F32)).astype(BF16)
    wr = wr_ref[...]
    w_hi = wr.astype(BF16)
    w_lo = (wr - w_hi.astype(F32)).astype(BF16)
    logits = (jnp.dot(a_hi, w_hi, preferred_element_type=F32)
              + jnp.dot(a_hi, w_lo, preferred_element_type=F32)
              + jnp.dot(a_lo, w_hi, preferred_element_type=F32))
    lane = lax.broadcasted_iota(jnp.int32, (tr, LANES), 1).astype(F32)
    neg = jnp.float32(-jnp.inf)
    logits = jnp.where(lane < N_EXPERTS, logits, neg)
    m1 = jnp.max(logits, axis=-1, keepdims=True)
    i1 = jnp.min(jnp.where(logits == m1, lane, float(LANES)), axis=-1, keepdims=True)
    rest = jnp.where(lane == i1, neg, logits)
    m2 = jnp.max(rest, axis=-1, keepdims=True)
    i2 = jnp.min(jnp.where(rest == m2, lane, float(LANES)), axis=-1, keepdims=True)
    e = jnp.exp(m2 - m1)
    g1 = 1.0 / (1.0 + e)
    g2 = e / (1.0 + e)
    pos = pl.program_id(1) * tr + lax.broadcasted_iota(jnp.int32, (tr, LANES), 0)
    valid = (pos >= PAD).astype(F32)
    oh1 = (lane == i1).astype(F32)
    oh2 = (lane == i2).astype(F32)
    chosen = (oh1 + oh2) * valid
    before = carry_scr[0:1, :] + jnp.dot(tri_ref[...], chosen.astype(BF16), preferred_element_type=F32)
    r1 = jnp.sum(before * oh1, axis=-1, keepdims=True)
    r2 = jnp.sum(before * oh2, axis=-1, keepdims=True)
    total = carry_scr[0:1, :] + jnp.sum(chosen, axis=0, keepdims=True)
    carry_scr[...] = jnp.broadcast_to(total, carry_scr.shape)
    cnt_ref[...] = jnp.broadcast_to(total, cnt_ref.shape)
    cols = (i1, i2, g1 * valid, g2 * valid, r1, r2)
    slab = jnp.zeros((tr, LANES), F32)
    for c, val in enumerate(cols):
        slab = jnp.where(lane == c, val, slab)
    slab_ref[...] = slab


def _router(h3, g, w_router):
    B, Lp, D = h3.shape
    tr = Lp // 2
    wr = jnp.zeros((D, LANES), F32).at[:, :N_EXPERTS].set(w_router)
    t = np.arange(tr)
    tri = (t[:, None] > t[None, :]).astype(np.float32)
    slab, cnt = pl.pallas_call(
        _router_kernel,
        grid=(B, Lp // tr),
        in_specs=[pl.BlockSpec((None, tr, D), lambda b, j: (b, j, 0)),
                  pl.BlockSpec((1, D), lambda b, j: (0, 0)),
                  pl.BlockSpec((D, LANES), lambda b, j: (0, 0)),
                  pl.BlockSpec((tr, tr), lambda b, j: (0, 0))],
        out_specs=[pl.BlockSpec((tr, LANES), lambda b, j: (b * (Lp // tr) + j, 0)),
                   pl.BlockSpec((8, LANES), lambda b, j: (0, 0))],
        out_shape=[jax.ShapeDtypeStruct((B * Lp, LANES), F32), jax.ShapeDtypeStruct((8, LANES), F32)],
        scratch_shapes=[pltpu.VMEM((8, LANES), F32)],
        compiler_params=_cparams("arbitrary", "arbitrary"),
        name="moe_router",
    )(h3, g.reshape(1, D), wr, jnp.asarray(tri, BF16))
    return slab, cnt[0, :N_EXPERTS]


def _invert_kernel(lo_ref, hi_ref, spare_ref, p1_hbm, p2_hbm, inv_ref, buf1, buf2, sem, *, n_tok):
    chunk = buf1.shape[0]
    tile_mask = spare_ref.shape[0] - 1

    def fill(r, carry):
        inv_ref[r] = spare_ref[r & tile_mask]
        return carry

    for e in range(N_EXPERTS + 1):
        lax.fori_loop(lo_ref[e], hi_ref[e], fill, 0)
    for c in range(n_tok // chunk):
        c1 = pltpu.make_async_copy(p1_hbm.at[pl.ds(c * chunk, chunk)], buf1, sem.at[0])
        c2 = pltpu.make_async_copy(p2_hbm.at[pl.ds(c * chunk, chunk)], buf2, sem.at[1])
        c1.start()
        c2.start()
        c1.wait()
        c2.wait()

        def token(i, carry, c=c):
            inv_ref[buf1[i]] = c * chunk + i
            inv_ref[buf2[i]] = n_tok + c * chunk + i
            return carry

        lax.fori_loop(0, chunk, token, 0, unroll=8)


def _invert(p1, p2, pad_lo, pad_hi, spare, n_rows):
    n_tok = p1.shape[0]
    chunk = _pick(n_tok, (2048, 1024, 512, 256, 128))
    grid_spec = pltpu.PrefetchScalarGridSpec(
        num_scalar_prefetch=3,
        grid=(1,),
        in_specs=[pl.BlockSpec(memory_space=pl.ANY), pl.BlockSpec(memory_space=pl.ANY)],
        out_specs=pl.BlockSpec(memory_space=pltpu.SMEM),
        scratch_shapes=[pltpu.SMEM((chunk,), jnp.int32), pltpu.SMEM((chunk,), jnp.int32),
                        pltpu.SemaphoreType.DMA((2,))])
    return pl.pallas_call(
        functools.partial(_invert_kernel, n_tok=n_tok),
        grid_spec=grid_spec,
        out_shape=jax.ShapeDtypeStruct((n_rows + 1,), jnp.int32),
        compiler_params=_cparams("arbitrary"),
        name="moe_invert",
    )(pad_lo, pad_hi, spare, p1, p2)


MOE_TILE = 512
MOE_GATES = 4


def _moe_ffn_kernel(te_ref, inv_ref, spare_ref, zero_ref, h_hbm, g_ref, w1_ref, w3_ref, w2_ref, y_hbm,
                    xbuf, ybuf, a_scr, gsem, ssem, *, n_tok, n_batch, seq_pad):
    del te_ref
    t = pl.program_id(0)
    n = pl.num_programs(0)
    tm = xbuf.shape[0]
    slot = t % 2
    other = 1 - slot

    def gather_row(tile, r, off=0):
        idx = inv_ref[tile * tm + r + off]
        src = jnp.where(idx >= n_tok, idx - n_tok, idx)
        pltpu.make_async_copy(h_hbm.at[pl.ds(src, 1)], xbuf.at[pl.ds(r, 1)], gsem.at[0]).start()

    def scatter_row(tile, buf_slot, r, off=0):
        dst = jnp.where(tile >= 0, inv_ref[jnp.maximum(tile, 0) * tm + r + off], spare_ref[r])
        pltpu.make_async_copy(ybuf.at[buf_slot, pl.ds(r, 1)], y_hbm.at[pl.ds(dst, 1)], ssem.at[buf_slot]).start()

    def wait_gather():
        pltpu.make_async_copy(h_hbm.at[pl.ds(0, tm)], xbuf, gsem.at[0]).wait()

    def wait_scatter(buf_slot):
        pltpu.make_async_copy(ybuf.at[buf_slot], y_hbm.at[pl.ds(0, tm)], ssem.at[buf_slot]).wait()

    @pl.when(t == 0)
    def _():
        ybuf[...] = jnp.zeros_like(ybuf)
        zero_copies = [pltpu.make_async_copy(ybuf.at[0, pl.ds(0, PAD)],
                                             y_hbm.at[pl.ds(k * n_tok + b * seq_pad, PAD)], ssem.at[0])
                       for k in range(2) for b in range(n_batch)]
        for cp in zero_copies:
            cp.start()
        for cp in zero_copies:
            cp.wait()

        def first(r, carry):
            gather_row(0, r)
            return carry

        lax.fori_loop(0, tm, first, 0, unroll=8)

    @pl.when(t > 0)
    def _():
        wait_scatter(slot)

    wait_gather()
    a_scr[...] = _rms(xbuf[...], g_ref[...]).astype(BF16)
    nxt = jnp.minimum(t + 1, n - 1)
    prv = t - 1
    n_chunks = w1_ref.shape[-1] // FFN_CHUNK
    bounds = [(c * tm) // n_chunks for c in range(n_chunks + 1)]
    for c in range(n_chunks):
        cs = slice(c * FFN_CHUNK, (c + 1) * FFN_CHUNK)
        a = a_scr[...]
        h1 = jnp.dot(a, w1_ref[0, :, cs], preferred_element_type=F32)
        h3 = jnp.dot(a, w3_ref[0, :, cs], preferred_element_type=F32)
        y = jnp.dot((_silu(h1) * h3).astype(BF16), w2_ref[0, cs, :], preferred_element_type=F32)
        if c == 0:
            ybuf[slot] = y
        else:
            ybuf[slot] += y
        gates = [res[rb * (tm // MOE_GATES):rb * (tm // MOE_GATES) + 1, 0:1]
                 for res in (h1, h3, y) for rb in range(MOE_GATES)]
        rows = list(range(bounds[c], bounds[c + 1]))
        for gi, gate in enumerate(gates):
            off = gate[0, 0].astype(jnp.int32) * zero_ref[0]
            for r in rows[(gi * len(rows)) // len(gates):((gi + 1) * len(rows)) // len(gates)]:
                gather_row(nxt, r, off)
                scatter_row(prv, other, r, off)

    @pl.when(t == n - 1)
    def _():
        def last(r, carry):
            scatter_row(t, slot, r)
            return carry

        wait_scatter(other)
        lax.fori_loop(0, tm, last, 0, unroll=8)
        wait_gather()
        wait_scatter(slot)


def _moe_ffn(h2, g, w1, w3, w2, tile_expert, inv, spare, n_rows, n_batch):
    R, D = h2.shape
    F = w1.shape[-1]
    tm = MOE_TILE
    kern = functools.partial(_moe_ffn_kernel, n_tok=R, n_batch=n_batch, seq_pad=R // n_batch)
    grid_spec = pltpu.PrefetchScalarGridSpec(
        num_scalar_prefetch=4,
        grid=(n_rows // tm,),
        in_specs=[pl.BlockSpec(memory_space=pl.ANY),
                  pl.BlockSpec((1, D), lambda t, te, inv, sp, z: (0, 0)),
                  pl.BlockSpec((1, D, F), lambda t, te, inv, sp, z: (te[t], 0, 0)),
                  pl.BlockSpec((1, D, F), lambda t, te, inv, sp, z: (te[t], 0, 0)),
                  pl.BlockSpec((1, F, D), lambda t, te, inv, sp, z: (te[t], 0, 0))],
        out_specs=pl.BlockSpec(memory_space=pl.ANY),
        scratch_shapes=[pltpu.VMEM((tm, D), F32), pltpu.VMEM((2, tm, D), F32), pltpu.VMEM((tm, D), BF16),
                        pltpu.SemaphoreType.DMA((1,)), pltpu.SemaphoreType.DMA((2,))])
    return pl.pallas_call(
        kern,
        grid_spec=grid_spec,
        out_shape=jax.ShapeDtypeStruct((2 * R, D), F32),
        compiler_params=_cparams("arbitrary"),
        name="swiglu_experts",
    )(tile_expert, inv, spare, jnp.zeros((1,), jnp.int32), h2, g.reshape(1, D), w1, w3, w2)


def _combine_kernel(h_ref, slab_ref, y1_ref, y2_ref, o_ref):
    slab = slab_ref[...]
    o_ref[...] = h_ref[...] + (slab[:, 2:3] * y1_ref[...] + slab[:, 3:4] * y2_ref[...])


def _combine(h2, slab, y):
    R, D = h2.shape
    tc = _pick(R, (512, 256, 128))
    return pl.pallas_call(
        _combine_kernel,
        grid=(R // tc,),
        in_specs=[pl.BlockSpec((tc, D), lambda i: (i, 0)),
                  pl.BlockSpec((tc, LANES), lambda i: (i, 0)),
                  pl.BlockSpec((tc, D), lambda i: (i, 0)),
                  pl.BlockSpec((tc, D), lambda i: (i + R // tc, 0))],
        out_specs=pl.BlockSpec((tc, D), lambda i: (i, 0)),
        out_shape=jax.ShapeDtypeStruct((R, D), F32),
        input_output_aliases={0: 0},
        compiler_params=_cparams("parallel"),
        name="moe_combine",
    )(h2, slab, y, y)


def _moe(h3, g, w_router, w1, w3, w2):
    B, Lp, D = h3.shape
    tm = MOE_TILE
    n_tokens = B * (Lp - PAD)
    n_tiles = (2 * n_tokens) // tm + N_EXPERTS
    n_rows = n_tiles * tm
    slab, counts = _router(h3, g, w_router)
    counts = counts.astype(jnp.int32)
    tiles_e = (counts + tm - 1) // tm
    end_tile = jnp.cumsum(tiles_e)
    offset = ((end_tile - tiles_e) * tm).astype(F32)
    n_used = end_tile[-1:]
    expert_ids = jnp.arange(N_EXPERTS, dtype=F32)
    e1, e2 = slab[:, 0:1], slab[:, 1:2]
    valid = (jnp.arange(B * Lp) % Lp) >= PAD
    off1 = jnp.sum(jnp.where(e1 == expert_ids[None, :], offset[None, :], 0.0), axis=-1)
    off2 = jnp.sum(jnp.where(e2 == expert_ids[None, :], offset[None, :], 0.0), axis=-1)
    p1 = jnp.where(valid, off1 + slab[:, 4], float(n_rows)).astype(jnp.int32)
    p2 = jnp.where(valid, off2 + slab[:, 5], float(n_rows)).astype(jnp.int32)
    tile_ids = jnp.arange(n_tiles, dtype=jnp.int32)
    last_tile = jnp.maximum(n_used - 1, 0)
    tile_expert = jnp.sum(jnp.minimum(tile_ids, last_tile)[:, None] >= end_tile[None, :], axis=-1).astype(jnp.int32)
    tile_expert = jnp.minimum(tile_expert, N_EXPERTS - 1)
    group_start = (end_tile - tiles_e) * tm
    pad_lo = jnp.concatenate([group_start + counts, n_used * tm]).astype(jnp.int32)
    pad_hi = jnp.concatenate([end_tile * tm, jnp.full((1,), n_rows + 1, jnp.int32)]).astype(jnp.int32)
    spare_rows = [k * B * Lp + b * Lp + o for k in range(2) for b in range(B) for o in range(PAD)]
    assert tm & (tm - 1) == 0 and len(spare_rows) >= tm
    spare = jnp.asarray(np.array(spare_rows[:tm], np.int32))
    inv = _invert(p1, p2, pad_lo, pad_hi, spare, n_rows)
    h2 = h3.reshape(B * Lp, D)
    ys = _moe_ffn(h2, g, w1, w3, w2, tile_expert, inv, spare, n_rows, B)
    return _combine(h2, slab, ys).reshape(B, Lp, D)


def _final_norm_kernel(h_ref, g_ref, o_ref):
    lead = h_ref.shape[0] - o_ref.shape[0]
    o_ref[...] = _rms(h_ref[lead:, :], g_ref[...])


def _final_norm(h3, g):
    B, Lp, D = h3.shape
    S = Lp - PAD - N_META
    return pl.pallas_call(
        _final_norm_kernel,
        grid=(B,),
        in_specs=[pl.BlockSpec((None, Lp, D), lambda b: (b, 0, 0)),
                  pl.BlockSpec((1, D), lambda b: (0, 0))],
        out_specs=pl.BlockSpec((None, S, D), lambda b: (b, 0, 0)),
        out_shape=jax.ShapeDtypeStruct((B, S, D), F32),
        compiler_params=_cparams("parallel"),
        name="final_norm",
    )(h3, g.reshape(1, D))


def kernel(x, meta_tokens, norm_mix_g, norm_ffn_g, ret_wq, ret_wk, ret_wv, ret_wg, ret_wo, ret_gn_g,
           sb_wqkv, sb_wo, pool_w, pool_scale, ffn_w1, ffn_w3, ffn_w2,
           moe_router, moe_w1, moe_w3, moe_w2, final_norm_g):
    B, S, D = x.shape
    Lp = PAD + N_META + S
    R = B * Lp
    depth = norm_mix_g.shape[0]
    meta = jnp.broadcast_to(meta_tokens.astype(x.dtype)[None], (B, N_META, D))
    h = jnp.concatenate([jnp.zeros((B, PAD, D), x.dtype), meta, x], axis=1)
    for i in range(depth):
        m, j = i % 3, i // 3
        if m == 0:
            w_in = jnp.concatenate([ret_wq[j], ret_wk[j], ret_wv[j], ret_wg[j]], axis=1).astype(BF16)
            p = _norm_matmul(h.reshape(R, D), norm_mix_g[i], w_in)
            y = _retention(p.reshape(B, Lp, -1), ret_gn_g[j])
            h = _matmul_residual(y.reshape(R, -1), ret_wo[j].astype(BF16), h.reshape(R, D)).reshape(B, Lp, D)
        elif m == 1:
            qkv = _norm_matmul(h.reshape(R, D), norm_mix_g[i], sb_wqkv[j].astype(BF16))
            y = _stick_breaking(qkv.reshape(B, Lp, -1))
            h = _matmul_residual(y.reshape(R, D), sb_wo[j].astype(BF16), h.reshape(R, D)).reshape(B, Lp, D)
        else:
            h = _pool_mixer(h, norm_mix_g[i], pool_w[j], pool_scale[j])
        c = i // 2
        if i % 2 == 0:
            h = _ffn(h.reshape(R, D), norm_ffn_g[i], ffn_w1[c].astype(BF16), ffn_w3[c].astype(BF16),
                     ffn_w2[c].astype(BF16)).reshape(B, Lp, D)
        else:
            h = _moe(h, norm_ffn_g[i], moe_router[c], moe_w1[c].astype(BF16), moe_w3[c].astype(BF16),
                     moe_w2[c].astype(BF16))
    return _final_norm(h, final_norm_g)
```

```python
import functools

import numpy as np
import jax
import jax.numpy as jnp
from jax import lax
from jax.experimental import pallas as pl
from jax.experimental.pallas import tpu as pltpu

F32 = jnp.float32
BF16 = jnp.bfloat16

D_MODEL = 1024
N_META = 16
BLOCK = 128
PAD = BLOCK - N_META
EPS = 1e-6
RET_HEADS = 8
RET_DK = D_MODEL // RET_HEADS
RET_DV = 2 * D_MODEL // RET_HEADS
ROPE_BASE = 10000.0
SB_HEADS = 16
SB_HD = D_MODEL // SB_HEADS
POOL_WINDOWS = (2, 4, 8, 16)
POOL_GD = D_MODEL // len(POOL_WINDOWS)
N_EXPERTS = 8
LANES = 128
VMEM_LIMIT_BYTES = 58 * 1024 * 1024


def _cparams(*sem):
    return pltpu.CompilerParams(dimension_semantics=sem, vmem_limit_bytes=VMEM_LIMIT_BYTES)


def _pick(n, cands):
    for c in cands:
        if n % c == 0:
            return c
    raise ValueError(f"no tile for {n} in {cands}")


def _rms(x, g):
    ms = jnp.mean(x * x, axis=-1, keepdims=True)
    return x * lax.rsqrt(ms + EPS) * g


def _silu(x):
    return x * (1.0 / (1.0 + jnp.exp(-x)))


def _norm_matmul_kernel(h_ref, g_ref, w_ref, o_ref, a_scr):
    @pl.when(pl.program_id(1) == 0)
    def _():
        a_scr[...] = _rms(h_ref[...], g_ref[...]).astype(BF16)

    o_ref[...] = jnp.dot(a_scr[...], w_ref[...], preferred_element_type=F32).astype(o_ref.dtype)


def _norm_matmul(h2, g, w):
    R, D = h2.shape
    N = w.shape[1]
    tm = _pick(R, (1024, 512, 256, 128))
    tn = _pick(N, (2048, 1536, 1024, 512, 256, 128))
    return pl.pallas_call(
        _norm_matmul_kernel,
        grid=(R // tm, N // tn),
        in_specs=[pl.BlockSpec((tm, D), lambda i, j: (i, 0)),
                  pl.BlockSpec((1, D), lambda i, j: (0, 0)),
                  pl.BlockSpec((D, tn), lambda i, j: (0, j))],
        out_specs=pl.BlockSpec((tm, tn), lambda i, j: (i, j)),
        out_shape=jax.ShapeDtypeStruct((R, N), BF16),
        scratch_shapes=[pltpu.VMEM((tm, D), BF16)],
        compiler_params=_cparams("parallel", "arbitrary"),
        name="norm_matmul",
    )(h2, g.reshape(1, D), w)


def _matmul_residual_kernel(y_ref, w_ref, h_ref, o_ref):
    o_ref[...] = h_ref[...] + jnp.dot(y_ref[...], w_ref[...], preferred_element_type=F32)


def _matmul_residual(y, w, h2):
    R, K = y.shape
    D = w.shape[1]
    tm = _pick(R, (512, 256, 128))
    return pl.pallas_call(
        _matmul_residual_kernel,
        grid=(R // tm,),
        in_specs=[pl.BlockSpec((tm, K), lambda i: (i, 0)),
                  pl.BlockSpec((K, D), lambda i: (0, 0)),
                  pl.BlockSpec((tm, D), lambda i: (i, 0))],
        out_specs=pl.BlockSpec((tm, D), lambda i: (i, 0)),
        out_shape=jax.ShapeDtypeStruct((R, D), F32),
        input_output_aliases={2: 0},
        compiler_params=_cparams("parallel"),
        name="matmul_residual",
    )(y, w, h2)


def _retention_kernel(p_ref, cos_ref, sin_ref, dec_ref, zeta_ref, xi_ref, cd_ref, gn_ref, o_ref, state_scr):
    @pl.when(pl.program_id(1) == 0)
    def _():
        state_scr[...] = jnp.zeros_like(state_scr)

    cos = cos_ref[...]
    sin = sin_ref[...]
    k0, v0, g0 = D_MODEL, 2 * D_MODEL, 4 * D_MODEL
    heads = range(RET_HEADS)
    vcol = [slice(v0 + h * RET_DV, v0 + (h + 1) * RET_DV) for h in heads]
    qb, kb, kz = [], [], []
    for h in heads:
        q = p_ref[:, h * RET_DK:(h + 1) * RET_DK].astype(F32)
        k = p_ref[:, k0 + h * RET_DK:k0 + (h + 1) * RET_DK].astype(F32)
        qr = q * cos + pltpu.roll(q, RET_DK // 2, 1) * sin
        kr = (k * cos + pltpu.roll(k, RET_DK // 2, 1) * sin) * (RET_DK ** -0.5)
        qb.append(qr.astype(BF16))
        kb.append(kr.astype(BF16))
        kz.append((kr * zeta_ref[h]).astype(BF16))
    s = [lax.dot_general(qb[h], kb[h], (((1,), (1,)), ((), ())), preferred_element_type=F32) for h in heads]
    cross = [jnp.dot(qb[h], state_scr[h].astype(BF16), preferred_element_type=F32) for h in heads]
    kv = [lax.dot_general(kz[h], p_ref[:, vcol[h]], (((0,), (0,)), ((), ())), preferred_element_type=F32)
          for h in heads]
    sb = [(s[h] * dec_ref[h]).astype(BF16) for h in heads]
    for h in heads:
        state_scr[h] = state_scr[h] * cd_ref[h] + kv[h]
    intra = [jnp.dot(sb[h], p_ref[:, vcol[h]], preferred_element_type=F32) for h in heads]
    for h in heads:
        y = intra[h] + cross[h] * xi_ref[h]
        mu = jnp.mean(y, axis=-1, keepdims=True)
        d = y - mu
        var = jnp.mean(d * d, axis=-1, keepdims=True)
        yn = d * lax.rsqrt(var + EPS) * gn_ref[:, h * RET_DV:(h + 1) * RET_DV]
        gate = p_ref[:, g0 + h * RET_DV:g0 + (h + 1) * RET_DV].astype(F32)
        o_ref[:, h * RET_DV:(h + 1) * RET_DV] = (_silu(gate) * yn).astype(o_ref.dtype)


def _retention_tables(Lp):
    half = RET_DK // 2
    inv = ROPE_BASE ** (-jnp.arange(half, dtype=F32) / half)
    pos = (jnp.arange(Lp) - PAD).astype(F32)
    ang = pos[:, None] * inv[None, :]
    cos, sin = jnp.cos(ang), jnp.sin(ang)
    cos2 = jnp.concatenate([cos, cos], axis=-1)
    sin2 = jnp.concatenate([-sin, sin], axis=-1)
    log_g = jnp.log1p(-jnp.exp2(-5.0 - jnp.arange(RET_HEADS, dtype=F32)))
    idx = jnp.arange(BLOCK, dtype=F32)
    diff = idx[:, None] - idx[None, :]
    decay = jnp.where(diff >= 0, jnp.exp(jnp.maximum(diff, 0.0)[None] * log_g[:, None, None]), 0.0)
    zeta = jnp.exp((BLOCK - 1 - idx)[None, :] * log_g[:, None])
    xi = jnp.exp((idx + 1)[None, :] * log_g[:, None])
    cd = jnp.exp(BLOCK * log_g)
    zeta_b = jnp.broadcast_to(zeta[:, :, None], (RET_HEADS, BLOCK, RET_DK))
    xi_b = jnp.broadcast_to(xi[:, :, None], (RET_HEADS, BLOCK, RET_DV))
    cd_b = jnp.broadcast_to(cd[:, None, None], (RET_HEADS, 1, RET_DV))
    return cos2, sin2, decay, zeta_b, xi_b, cd_b


def _retention(p3, gn_g):
    B, Lp, W = p3.shape
    cos2, sin2, decay, zeta_b, xi_b, cd_b = _retention_tables(Lp)
    const3 = lambda b, c: (0, 0, 0)
    return pl.pallas_call(
        _retention_kernel,
        grid=(B, Lp // BLOCK),
        in_specs=[pl.BlockSpec((None, BLOCK, W), lambda b, c: (b, c, 0)),
                  pl.BlockSpec((BLOCK, RET_DK), lambda b, c: (c, 0)),
                  pl.BlockSpec((BLOCK, RET_DK), lambda b, c: (c, 0)),
                  pl.BlockSpec((RET_HEADS, BLOCK, BLOCK), const3),
                  pl.BlockSpec((RET_HEADS, BLOCK, RET_DK), const3),
                  pl.BlockSpec((RET_HEADS, BLOCK, RET_DV), const3),
                  pl.BlockSpec((RET_HEADS, 1, RET_DV), const3),
                  pl.BlockSpec((1, 2 * D_MODEL), lambda b, c: (0, 0))],
        out_specs=pl.BlockSpec((None, BLOCK, 2 * D_MODEL), lambda b, c: (b, c, 0)),
        out_shape=jax.ShapeDtypeStruct((B, Lp, 2 * D_MODEL), BF16),
        scratch_shapes=[pltpu.VMEM((RET_HEADS, RET_DK, RET_DV), F32)],
        compiler_params=_cparams("parallel", "arbitrary"),
        name="retention",
    )(p3, cos2, sin2, decay, zeta_b, xi_b, cd_b, gn_g.reshape(1, 2 * D_MODEL))


SB_PAIRS = D_MODEL // LANES
SB_BLOCKS_PER_TRIP = 2


def _sb_kernel(q_ref, k_ref, v_ref, u_ref, o_ref, q_scr, acc_scr, car_scr):
    i = pl.program_id(1)
    lane = lax.broadcasted_iota(jnp.int32, (BLOCK, LANES), 1)
    for p in range(SB_PAIRS):
        qs = q_ref[:, p * LANES:(p + 1) * LANES] * (SB_HD ** -0.5)
        zero = jnp.zeros_like(qs)
        q_scr[p, :BLOCK] = jnp.where(lane < SB_HD, qs, zero)
        q_scr[p, BLOCK:] = jnp.where(lane >= SB_HD, qs, zero)
    row = lax.broadcasted_iota(jnp.int32, (2 * BLOCK, BLOCK), 0)
    col = lax.broadcasted_iota(jnp.int32, (2 * BLOCK, BLOCK), 1)
    qpos = i * BLOCK + (row & (BLOCK - 1))
    acc_scr[...] = jnp.zeros_like(acc_scr)
    car_scr[...] = jnp.zeros_like(car_scr)

    def key_blocks(js, masked):
        rows = [pl.ds(pl.multiple_of(j * BLOCK, BLOCK), BLOCK) for j in js]
        if masked:
            masks = [((j * BLOCK + col) < qpos) & ((j * BLOCK + col) >= PAD) for j in js]
            sel = lambda b, x: jnp.where(masks[b], x, 0.0)
        else:
            sel = lambda b, x: x
        items = [(b, p) for b in range(len(js)) for p in range(SB_PAIRS)]
        cols = [slice(p * LANES, (p + 1) * LANES) for p in range(SB_PAIRS)]
        z = {(b, p): lax.dot_general(q_scr[p], k_ref[rows[b], cols[p]], (((1,), (1,)), ((), ())),
                                     preferred_element_type=F32) for b, p in items}
        log_beta = {k: jnp.minimum(z[k], 0.0) - jnp.log(1.0 + jnp.exp(-jnp.abs(z[k]))) for k in items}
        log_1m = {(b, p): sel(b, log_beta[b, p] - z[b, p]).astype(BF16) for b, p in items}
        r = {k: jnp.dot(log_1m[k], u_ref[...], preferred_element_type=F32) for k in items}
        after = {p: car_scr[p] for p in range(SB_PAIRS)}
        w = {}
        for b, p in items:
            w[b, p] = sel(b, jnp.exp(log_beta[b, p] + (after[p] + r[b, p][:, :BLOCK]))).astype(BF16)
            after[p] = after[p] + r[b, p][:, BLOCK:]
        o = {(b, p): jnp.dot(w[b, p], v_ref[rows[b], cols[p]], preferred_element_type=F32) for b, p in items}
        for p in range(SB_PAIRS):
            acc_scr[p] += sum(o[b, p] for b in range(len(js)))
            car_scr[p] = after[p]

    key_blocks([i], True)

    n_int = jnp.maximum(i - 1, 0)
    n_trips = n_int // SB_BLOCKS_PER_TRIP

    def interior(t, carry):
        j = i - 1 - SB_BLOCKS_PER_TRIP * t
        key_blocks([j - b for b in range(SB_BLOCKS_PER_TRIP)], False)
        return carry

    lax.fori_loop(0, n_trips, interior, 0)

    def leftover(t, carry):
        key_blocks([n_int - SB_BLOCKS_PER_TRIP * n_trips - t], False)
        return carry

    lax.fori_loop(0, n_int - SB_BLOCKS_PER_TRIP * n_trips, leftover, 0)

    @pl.when(i > 0)
    def _():
        key_blocks([0], True)
    for p in range(SB_PAIRS):
        o_ref[:, p * LANES:(p + 1) * LANES] = jnp.where(
            lane < SB_HD, acc_scr[p, :BLOCK], acc_scr[p, BLOCK:]).astype(o_ref.dtype)


def _stick_breaking(qkv3):
    B, Lp, _ = qkv3.shape
    j_idx = np.arange(BLOCK)
    u = np.concatenate([(j_idx[:, None] > j_idx[None, :]).astype(np.float32),
                        np.ones((BLOCK, BLOCK), np.float32)], axis=1)
    return pl.pallas_call(
        _sb_kernel,
        grid=(B, Lp // BLOCK),
        in_specs=[pl.BlockSpec((None, BLOCK, D_MODEL), lambda b, i: (b, i, 0)),
                  pl.BlockSpec((None, Lp, D_MODEL), lambda b, i: (b, 0, 1)),
                  pl.BlockSpec((None, Lp, D_MODEL), lambda b, i: (b, 0, 2)),
                  pl.BlockSpec((BLOCK, 2 * BLOCK), lambda b, i: (0, 0))],
        out_specs=pl.BlockSpec((None, BLOCK, D_MODEL), lambda b, i: (b, i, 0)),
        out_shape=jax.ShapeDtypeStruct((B, Lp, D_MODEL), BF16),
        scratch_shapes=[pltpu.VMEM((SB_PAIRS, 2 * BLOCK, LANES), BF16),
                        pltpu.VMEM((SB_PAIRS, 2 * BLOCK, LANES), F32),
                        pltpu.VMEM((SB_PAIRS, 2 * BLOCK, BLOCK), F32)],
        compiler_params=_cparams("parallel", "arbitrary"),
        name="stick_breaking",
    )(qkv3, qkv3, qkv3, jnp.asarray(u, BF16))


def _pool_kernel(h_ref, g_ref, bm_ref, bh_ref, w_ref, sc_ref, o_ref, prev_scr):
    prev_scr[...] = jnp.zeros_like(prev_scr)

    def tile(j, carry):
        rows = pl.ds(pl.multiple_of(j * BLOCK, BLOCK), BLOCK)
        x = h_ref[rows, :]
        pos = j * BLOCK + lax.broadcasted_iota(jnp.int32, (BLOCK, POOL_GD), 0) - PAD
        a = _rms(x, g_ref[...])
        a_hi = a.astype(BF16)
        a_lo = (a - a_hi.astype(F32)).astype(BF16)
        for g, win in enumerate(POOL_WINDOWS):
            sl = slice(g * POOL_GD, (g + 1) * POOL_GD)
            ws = (jnp.dot(bm_ref[g], a_hi[:, sl], preferred_element_type=F32)
                  + jnp.dot(bm_ref[g], a_lo[:, sl], preferred_element_type=F32)
                  + jnp.dot(bh_ref[g], prev_scr[0, :, sl], preferred_element_type=F32)
                  + jnp.dot(bh_ref[g], prev_scr[1, :, sl], preferred_element_type=F32))
            cnt = jnp.clip(pos + 1, 1, win).astype(F32)
            diff = ws / cnt - a[:, sl]
            y = jnp.dot(diff.astype(BF16), w_ref[g], preferred_element_type=F32) * sc_ref[:, sl]
            o_ref[rows, sl] = x[:, sl] + y
        prev_scr[0] = a_hi
        prev_scr[1] = a_lo
        return carry

    lax.fori_loop(0, h_ref.shape[0] // BLOCK, tile, 0, unroll=2)


def _pool_mixer(h3, g, w_pool, scale):
    B, Lp, D = h3.shape
    t = np.arange(BLOCK)
    d_main = t[:, None] - t[None, :]
    d_halo = t[:, None] + BLOCK - t[None, :]
    bm = np.stack([((d_main >= 0) & (d_main < w)) for w in POOL_WINDOWS]).astype(np.float32)
    bh = np.stack([(d_halo < w) for w in POOL_WINDOWS]).astype(np.float32)
    ng = len(POOL_WINDOWS)
    return pl.pallas_call(
        _pool_kernel,
        grid=(B,),
        in_specs=[pl.BlockSpec((None, Lp, D), lambda b: (b, 0, 0)),
                  pl.BlockSpec((1, D), lambda b: (0, 0)),
                  pl.BlockSpec((ng, BLOCK, BLOCK), lambda b: (0, 0, 0)),
                  pl.BlockSpec((ng, BLOCK, BLOCK), lambda b: (0, 0, 0)),
                  pl.BlockSpec((ng, POOL_GD, POOL_GD), lambda b: (0, 0, 0)),
                  pl.BlockSpec((1, D), lambda b: (0, 0))],
        out_specs=pl.BlockSpec((None, Lp, D), lambda b: (b, 0, 0)),
        out_shape=jax.ShapeDtypeStruct((B, Lp, D), F32),
        scratch_shapes=[pltpu.VMEM((2, BLOCK, D), BF16)],
        input_output_aliases={0: 0},
        compiler_params=_cparams("parallel"),
        name="pool_mixer",
    )(h3, g.reshape(1, D), jnp.asarray(bm, BF16), jnp.asarray(bh, BF16), w_pool.astype(BF16),
      scale.reshape(1, D))


FFN_CHUNK = 256


def _ffn_kernel(x_ref, g_ref, w1_ref, w3_ref, w2_ref, o_ref, a_scr):
    x = x_ref[...]
    a_scr[...] = _rms(x, g_ref[...]).astype(BF16)
    for c in range(w1_ref.shape[-1] // FFN_CHUNK):
        cs = slice(c * FFN_CHUNK, (c + 1) * FFN_CHUNK)
        a = a_scr[...]
        h1 = jnp.dot(a, w1_ref[:, cs], preferred_element_type=F32)
        h3 = jnp.dot(a, w3_ref[:, cs], preferred_element_type=F32)
        y = jnp.dot((_silu(h1) * h3).astype(BF16), w2_ref[cs, :], preferred_element_type=F32)
        if c == 0:
            o_ref[...] = x + y
        else:
            o_ref[...] += y


def _ffn(x2, g, w1, w3, w2):
    R, D = x2.shape
    F = w1.shape[-1]
    tm = _pick(R, (512, 256, 128))
    return pl.pallas_call(
        _ffn_kernel,
        grid=(R // tm,),
        in_specs=[pl.BlockSpec((tm, D), lambda t: (t, 0)),
                  pl.BlockSpec((1, D), lambda t: (0, 0)),
                  pl.BlockSpec((D, F), lambda t: (0, 0)),
                  pl.BlockSpec((D, F), lambda t: (0, 0)),
                  pl.BlockSpec((F, D), lambda t: (0, 0))],
        out_specs=pl.BlockSpec((tm, D), lambda t: (t, 0)),
        out_shape=jax.ShapeDtypeStruct((R, D), F32),
        scratch_shapes=[pltpu.VMEM((tm, D), BF16)],
        input_output_aliases={0: 0},
        compiler_params=_cparams("parallel"),
        name="swiglu_dense",
    )(x2, g.reshape(1, D), w1, w3, w2)


def _router_kernel(h_ref, g_ref, wr_ref, tri_ref, slab_ref, cnt_ref, carry_scr):
    first = (pl.program_id(0) == 0) & (pl.program_id(1) == 0)

    @pl.when(first)
    def _():
        carry_scr[...] = jnp.zeros_like(carry_scr)

    tr = h_ref.shape[0]
    a = _rms(h_ref[...], g_ref[...])
    a_hi = a.astype(BF16)
    a_lo = (a - a_hi.astype(F32)).astype(BF16)
    wr = wr_ref[...]
    w_hi = wr.astype(BF16)
    w_lo = (wr - w_hi.astype(F32)).astype(BF16)
    logits = (jnp.dot(a_hi, w_hi, preferred_element_type=F32)
              + jnp.dot(a_hi, w_lo, preferred_element_type=F32)
              + jnp.dot(a_lo, w_hi, preferred_element_type=F32))
    lane = lax.broadcasted_iota(jnp.int32, (tr, LANES), 1).astype(F32)
    neg = jnp.float32(-jnp.inf)
    logits = jnp.where(lane < N_EXPERTS, logits, neg)
    m1 = jnp.max(logits, axis=-1, keepdims=True)
    i1 = jnp.min(jnp.where(logits == m1, lane, float(LANES)), axis=-1, keepdims=True)
    rest = jnp.where(lane == i1, neg, logits)
    m2 = jnp.max(rest, axis=-1, keepdims=True)
    i2 = jnp.min(jnp.where(rest == m2, lane, float(LANES)), axis=-1, keepdims=True)
    e = jnp.exp(m2 - m1)
    g1 = 1.0 / (1.0 + e)
    g2 = e / (1.0 + e)
    pos = pl.program_id(1) * tr + lax.broadcasted_iota(jnp.int32, (tr, LANES), 0)
    valid = (pos >= PAD).astype(F32)
    oh1 = (lane == i1).astype(F32)
    oh2 = (lane == i2).astype(F32)
    chosen = (oh1 + oh2) * valid
    before = carry_scr[0:1, :] + jnp.dot(tri_ref[...], chosen.astype(BF16), preferred_element_type=F32)
    r1 = jnp.sum(before * oh1, axis=-1, keepdims=True)
    r2 = jnp.sum(before * oh2, axis=-1, keepdims=True)
    total = carry_scr[0:1, :] + jnp.sum(chosen, axis=0, keepdims=True)
    carry_scr[...] = jnp.broadcast_to(total, carry_scr.shape)
    cnt_ref[...] = jnp.broadcast_to(total, cnt_ref.shape)
    cols = (i1, i2, g1 * valid, g2 * valid, r1, r2)
    slab = jnp.zeros((tr, LANES), F32)
    for c, val in enumerate(cols):
        slab = jnp.where(lane == c, val, slab)
    slab_ref[...] = slab


def _router(h3, g, w_router):
    B, Lp, D = h3.shape
    tr = Lp // 2
    wr = jnp.zeros((D, LANES), F32).at[:, :N_EXPERTS].set(w_router)
    t = np.arange(tr)
    tri = (t[:, None] > t[None, :]).astype(np.float32)
    slab, cnt = pl.pallas_call(
        _router_kernel,
        grid=(B, Lp // tr),
        in_specs=[pl.BlockSpec((None, tr, D), lambda b, j: (b, j, 0)),
                  pl.BlockSpec((1, D), lambda b, j: (0, 0)),
                  pl.BlockSpec((D, LANES), lambda b, j: (0, 0)),
                  pl.BlockSpec((tr, tr), lambda b, j: (0, 0))],
        out_specs=[pl.BlockSpec((tr, LANES), lambda b, j: (b * (Lp // tr) + j, 0)),
                   pl.BlockSpec((8, LANES), lambda b, j: (0, 0))],
        out_shape=[jax.ShapeDtypeStruct((B * Lp, LANES), F32), jax.ShapeDtypeStruct((8, LANES), F32)],
        scratch_shapes=[pltpu.VMEM((8, LANES), F32)],
        compiler_params=_cparams("arbitrary", "arbitrary"),
        name="moe_router",
    )(h3, g.reshape(1, D), wr, jnp.asarray(tri, BF16))
    return slab, cnt[0, :N_EXPERTS]


def _invert_kernel(lo_ref, hi_ref, spare_ref, p1_hbm, p2_hbm, inv_ref, buf1, buf2, sem, *, n_tok):
    chunk = buf1.shape[0]
    tile_mask = spare_ref.shape[0] - 1

    def fill(r, carry):
        inv_ref[r] = spare_ref[r & tile_mask]
        return carry

    for e in range(N_EXPERTS + 1):
        lax.fori_loop(lo_ref[e], hi_ref[e], fill, 0)
    for c in range(n_tok // chunk):
        c1 = pltpu.make_async_copy(p1_hbm.at[pl.ds(c * chunk, chunk)], buf1, sem.at[0])
        c2 = pltpu.make_async_copy(p2_hbm.at[pl.ds(c * chunk, chunk)], buf2, sem.at[1])
        c1.start()
        c2.start()
        c1.wait()
        c2.wait()

        def token(i, carry, c=c):
            inv_ref[buf1[i]] = c * chunk + i
            inv_ref[buf2[i]] = n_tok + c * chunk + i
            return carry

        lax.fori_loop(0, chunk, token, 0, unroll=8)


def _invert(p1, p2, pad_lo, pad_hi, spare, n_rows):
    n_tok = p1.shape[0]
    chunk = _pick(n_tok, (2048, 1024, 512, 256, 128))
    grid_spec = pltpu.PrefetchScalarGridSpec(
        num_scalar_prefetch=3,
        grid=(1,),
        in_specs=[pl.BlockSpec(memory_space=pl.ANY), pl.BlockSpec(memory_space=pl.ANY)],
        out_specs=pl.BlockSpec(memory_space=pltpu.SMEM),
        scratch_shapes=[pltpu.SMEM((chunk,), jnp.int32), pltpu.SMEM((chunk,), jnp.int32),
                        pltpu.SemaphoreType.DMA((2,))])
    return pl.pallas_call(
        functools.partial(_invert_kernel, n_tok=n_tok),
        grid_spec=grid_spec,
        out_shape=jax.ShapeDtypeStruct((n_rows + 1,), jnp.int32),
        compiler_params=_cparams("arbitrary"),
        name="moe_invert",
    )(pad_lo, pad_hi, spare, p1, p2)


MOE_TILE = 512
MOE_GATES = 4


def _moe_ffn_kernel(te_ref, inv_ref, spare_ref, zero_ref, h_hbm, g_ref, w1_ref, w3_ref, w2_ref, y_hbm,
                    xbuf, ybuf, a_scr, gsem, ssem, *, n_tok, n_batch, seq_pad):
    del te_ref
    t = pl.program_id(0)
    n = pl.num_programs(0)
    tm = xbuf.shape[0]
    slot = t % 2
    other = 1 - slot

    def gather_row(tile, r, off=0):
        idx = inv_ref[tile * tm + r + off]
        src = jnp.where(idx >= n_tok, idx - n_tok, idx)
        pltpu.make_async_copy(h_hbm.at[pl.ds(src, 1)], xbuf.at[pl.ds(r, 1)], gsem.at[0]).start()

    def scatter_row(tile, buf_slot, r, off=0):
        dst = jnp.where(tile >= 0, inv_ref[jnp.maximum(tile, 0) * tm + r + off], spare_ref[r])
        pltpu.make_async_copy(ybuf.at[buf_slot, pl.ds(r, 1)], y_hbm.at[pl.ds(dst, 1)], ssem.at[buf_slot]).start()

    def wait_gather():
        pltpu.make_async_copy(h_hbm.at[pl.ds(0, tm)], xbuf, gsem.at[0]).wait()

    def wait_scatter(buf_slot):
        pltpu.make_async_copy(ybuf.at[buf_slot], y_hbm.at[pl.ds(0, tm)], ssem.at[buf_slot]).wait()

    @pl.when(t == 0)
    def _():
        ybuf[...] = jnp.zeros_like(ybuf)
        zero_copies = [pltpu.make_async_copy(ybuf.at[0, pl.ds(0, PAD)],
                                             y_hbm.at[pl.ds(k * n_tok + b * seq_pad, PAD)], ssem.at[0])
                       for k in range(2) for b in range(n_batch)]
        for cp in zero_copies:
            cp.start()
        for cp in zero_copies:
            cp.wait()

        def first(r, carry):
            gather_row(0, r)
            return carry

        lax.fori_loop(0, tm, first, 0, unroll=8)

    @pl.when(t > 0)
    def _():
        wait_scatter(slot)

    wait_gather()
    a_scr[...] = _rms(xbuf[...], g_ref[...]).astype(BF16)
    nxt = jnp.minimum(t + 1, n - 1)
    prv = t - 1
    n_chunks = w1_ref.shape[-1] // FFN_CHUNK
    bounds = [(c * tm) // n_chunks for c in range(n_chunks + 1)]
    for c in range(n_chunks):
        cs = slice(c * FFN_CHUNK, (c + 1) * FFN_CHUNK)
        kh = a_scr.shape[1] // 2
        a_lo, a_hi = a_scr[:, :kh], a_scr[:, kh:]
        p1 = jnp.dot(a_lo, w1_ref[0, :kh, cs], preferred_element_type=F32)
        h1 = p1 + jnp.dot(a_hi, w1_ref[0, kh:, cs], preferred_element_type=F32)
        p3 = jnp.dot(a_lo, w3_ref[0, :kh, cs], preferred_element_type=F32)
        h3 = p3 + jnp.dot(a_hi, w3_ref[0, kh:, cs], preferred_element_type=F32)
        y = jnp.dot((_silu(h1) * h3).astype(BF16), w2_ref[0, cs, :], preferred_element_type=F32)
        if c == 0:
            ybuf[slot] = y
        else:
            ybuf[slot] += y
        gates = [res[rb * (tm // MOE_GATES):rb * (tm // MOE_GATES) + 1, 0:1]
                 for res in (p1, h1, p3, h3, y) for rb in range(MOE_GATES)]
        rows = list(range(bounds[c], bounds[c + 1]))
        for gi, gate in enumerate(gates):
            off = gate[0, 0].astype(jnp.int32) * zero_ref[0]
            for r in rows[(gi * len(rows)) // len(gates):((gi + 1) * len(rows)) // len(gates)]:
                gather_row(nxt, r, off)
                scatter_row(prv, other, r, off)

    @pl.when(t == n - 1)
    def _():
        def last(r, carry):
            scatter_row(t, slot, r)
            return carry

        wait_scatter(other)
        lax.fori_loop(0, tm, last, 0, unroll=8)
        wait_gather()
        wait_scatter(slot)


def _moe_ffn(h2, g, w1, w3, w2, tile_expert, inv, spare, n_rows, n_batch):
    R, D = h2.shape
    F = w1.shape[-1]
    tm = MOE_TILE
    kern = functools.partial(_moe_ffn_kernel, n_tok=R, n_batch=n_batch, seq_pad=R // n_batch)
    grid_spec = pltpu.PrefetchScalarGridSpec(
        num_scalar_prefetch=4,
        grid=(n_rows // tm,),
        in_specs=[pl.BlockSpec(memory_space=pl.ANY),
                  pl.BlockSpec((1, D), lambda t, te, inv, sp, z: (0, 0)),
                  pl.BlockSpec((1, D, F), lambda t, te, inv, sp, z: (te[t], 0, 0)),
                  pl.BlockSpec((1, D, F), lambda t, te, inv, sp, z: (te[t], 0, 0)),
                  pl.BlockSpec((1, F, D), lambda t, te, inv, sp, z: (te[t], 0, 0))],
        out_specs=pl.BlockSpec(memory_space=pl.ANY),
        scratch_shapes=[pltpu.VMEM((tm, D), F32), pltpu.VMEM((2, tm, D), F32), pltpu.VMEM((tm, D), BF16),
                        pltpu.SemaphoreType.DMA((1,)), pltpu.SemaphoreType.DMA((2,))])
    return pl.pallas_call(
        kern,
        grid_spec=grid_spec,
        out_shape=jax.ShapeDtypeStruct((2 * R, D), F32),
        compiler_params=_cparams("arbitrary"),
        name="swiglu_experts",
    )(tile_expert, inv, spare, jnp.zeros((1,), jnp.int32), h2, g.reshape(1, D), w1, w3, w2)


def _combine_kernel(h_ref, slab_ref, y1_ref, y2_ref, o_ref):
    slab = slab_ref[...]
    o_ref[...] = h_ref[...] + (slab[:, 2:3] * y1_ref[...] + slab[:, 3:4] * y2_ref[...])


def _combine(h2, slab, y):
    R, D = h2.shape
    tc = _pick(R, (512, 256, 128))
    return pl.pallas_call(
        _combine_kernel,
        grid=(R // tc,),
        in_specs=[pl.BlockSpec((tc, D), lambda i: (i, 0)),
                  pl.BlockSpec((tc, LANES), lambda i: (i, 0)),
                  pl.BlockSpec((tc, D), lambda i: (i, 0)),
                  pl.BlockSpec((tc, D), lambda i: (i + R // tc, 0))],
        out_specs=pl.BlockSpec((tc, D), lambda i: (i, 0)),
        out_shape=jax.ShapeDtypeStruct((R, D), F32),
        input_output_aliases={0: 0},
        compiler_params=_cparams("parallel"),
        name="moe_combine",
    )(h2, slab, y, y)


def _moe(h3, g, w_router, w1, w3, w2):
    B, Lp, D = h3.shape
    tm = MOE_TILE
    n_tokens = B * (Lp - PAD)
    n_tiles = (2 * n_tokens) // tm + N_EXPERTS
    n_rows = n_tiles * tm
    slab, counts = _router(h3, g, w_router)
    counts = counts.astype(jnp.int32)
    tiles_e = (counts + tm - 1) // tm
    end_tile = jnp.cumsum(tiles_e)
    offset = ((end_tile - tiles_e) * tm).astype(F32)
    n_used = end_tile[-1:]
    expert_ids = jnp.arange(N_EXPERTS, dtype=F32)
    e1, e2 = slab[:, 0:1], slab[:, 1:2]
    valid = (jnp.arange(B * Lp) % Lp) >= PAD
    off1 = jnp.sum(jnp.where(e1 == expert_ids[None, :], offset[None, :], 0.0), axis=-1)
    off2 = jnp.sum(jnp.where(e2 == expert_ids[None, :], offset[None, :], 0.0), axis=-1)
    p1 = jnp.where(valid, off1 + slab[:, 4], float(n_rows)).astype(jnp.int32)
    p2 = jnp.where(valid, off2 + slab[:, 5], float(n_rows)).astype(jnp.int32)
    tile_ids = jnp.arange(n_tiles, dtype=jnp.int32)
    last_tile = jnp.maximum(n_used - 1, 0)
    tile_expert = jnp.sum(jnp.minimum(tile_ids, last_tile)[:, None] >= end_tile[None, :], axis=-1).astype(jnp.int32)
    tile_expert = jnp.minimum(tile_expert, N_EXPERTS - 1)
    group_start = (end_tile - tiles_e) * tm
    pad_lo = jnp.concatenate([group_start + counts, n_used * tm]).astype(jnp.int32)
    pad_hi = jnp.concatenate([end_tile * tm, jnp.full((1,), n_rows + 1, jnp.int32)]).astype(jnp.int32)
    spare_rows = [k * B * Lp + b * Lp + o for k in range(2) for b in range(B) for o in range(PAD)]
    assert tm & (tm - 1) == 0 and len(spare_rows) >= tm
    spare = jnp.asarray(np.array(spare_rows[:tm], np.int32))
    inv = _invert(p1, p2, pad_lo, pad_hi, spare, n_rows)
    h2 = h3.reshape(B * Lp, D)
    ys = _moe_ffn(h2, g, w1, w3, w2, tile_expert, inv, spare, n_rows, B)
    return _combine(h2, slab, ys).reshape(B, Lp, D)


def _final_norm_kernel(h_ref, g_ref, o_ref):
    lead = h_ref.shape[0] - o_ref.shape[0]
    o_ref[...] = _rms(h_ref[lead:, :], g_ref[...])


def _final_norm(h3, g):
    B, Lp, D = h3.shape
    S = Lp - PAD - N_META
    return pl.pallas_call(
        _final_norm_kernel,
        grid=(B,),
        in_specs=[pl.BlockSpec((None, Lp, D), lambda b: (b, 0, 0)),
                  pl.BlockSpec((1, D), lambda b: (0, 0))],
        out_specs=pl.BlockSpec((None, S, D), lambda b: (b, 0, 0)),
        out_shape=jax.ShapeDtypeStruct((B, S, D), F32),
        compiler_params=_cparams("parallel"),
        name="final_norm",
    )(h3, g.reshape(1, D))


def kernel(x, meta_tokens, norm_mix_g, norm_ffn_g, ret_wq, ret_wk, ret_wv, ret_wg, ret_wo, ret_gn_g,
           sb_wqkv, sb_wo, pool_w, pool_scale, ffn_w1, ffn_w3, ffn_w2,
           moe_router, moe_w1, moe_w3, moe_w2, final_norm_g):
    B, S, D = x.shape
    Lp = PAD + N_META + S
    R = B * Lp
    depth = norm_mix_g.shape[0]
    meta = jnp.broadcast_to(meta_tokens.astype(x.dtype)[None], (B, N_META, D))
    h = jnp.concatenate([jnp.zeros((B, PAD, D), x.dtype), meta, x], axis=1)
    for i in range(depth):
        m, j = i % 3, i // 3
        if m == 0:
            w_in = jnp.concatenate([ret_wq[j], ret_wk[j], ret_wv[j], ret_wg[j]], axis=1).astype(BF16)
            p = _norm_matmul(h.reshape(R, D), norm_mix_g[i], w_in)
            y = _retention(p.reshape(B, Lp, -1), ret_gn_g[j])
            h = _matmul_residual(y.reshape(R, -1), ret_wo[j].astype(BF16), h.reshape(R, D)).reshape(B, Lp, D)
        elif m == 1:
            qkv = _norm_matmul(h.reshape(R, D), norm_mix_g[i], sb_wqkv[j].astype(BF16))
            y = _stick_breaking(qkv.reshape(B, Lp, -1))
            h = _matmul_residual(y.reshape(R, D), sb_wo[j].astype(BF16), h.reshape(R, D)).reshape(B, Lp, D)
        else:
            h = _pool_mixer(h, norm_mix_g[i], pool_w[j], pool_scale[j])
        c = i // 2
        if i % 2 == 0:
            h = _ffn(h.reshape(R, D), norm_ffn_g[i], ffn_w1[c].astype(BF16), ffn_w3[c].astype(BF16),
                     ffn_w2[c].astype(BF16)).reshape(B, Lp, D)
        else:
            h = _moe(h, norm_ffn_g[i], moe_router[c], moe_w1[c].astype(BF16), moe_w3[c].astype(BF16),
                     moe_w2[c].astype(BF16))
    return _final_norm(h, final_norm_g)
```

```python
import functools

import numpy as np
import jax
import jax.numpy as jnp
from jax import lax
from jax.experimental import pallas as pl
from jax.experimental.pallas import tpu as pltpu

F32 = jnp.float32
BF16 = jnp.bfloat16

D_MODEL = 1024
N_META = 16
BLOCK = 128
PAD = BLOCK - N_META
EPS = 1e-6
RET_HEADS = 8
RET_DK = D_MODEL // RET_HEADS
RET_DV = 2 * D_MODEL // RET_HEADS
ROPE_BASE = 10000.0
SB_HEADS = 16
SB_HD = D_MODEL // SB_HEADS
POOL_WINDOWS = (2, 4, 8, 16)
POOL_GD = D_MODEL // len(POOL_WINDOWS)
N_EXPERTS = 8
LANES = 128
VMEM_LIMIT_BYTES = 58 * 1024 * 1024


def _cparams(*sem):
    return pltpu.CompilerParams(dimension_semantics=sem, vmem_limit_bytes=VMEM_LIMIT_BYTES)


def _pick(n, cands):
    for c in cands:
        if n % c == 0:
            return c
    raise ValueError(f"no tile for {n} in {cands}")


def _rms(x, g):
    ms = jnp.mean(x * x, axis=-1, keepdims=True)
    return x * lax.rsqrt(ms + EPS) * g


def _silu(x):
    return x * (1.0 / (1.0 + jnp.exp(-x)))


def _norm_matmul_kernel(h_ref, g_ref, w_ref, o_ref, a_scr):
    @pl.when(pl.program_id(1) == 0)
    def _():
        a_scr[...] = _rms(h_ref[...], g_ref[...]).astype(BF16)

    o_ref[...] = jnp.dot(a_scr[...], w_ref[...], preferred_element_type=F32).astype(o_ref.dtype)


def _norm_matmul(h2, g, w):
    R, D = h2.shape
    N = w.shape[1]
    tm = _pick(R, (1024, 512, 256, 128))
    tn = _pick(N, (2048, 1536, 1024, 512, 256, 128))
    return pl.pallas_call(
        _norm_matmul_kernel,
        grid=(R // tm, N // tn),
        in_specs=[pl.BlockSpec((tm, D), lambda i, j: (i, 0)),
                  pl.BlockSpec((1, D), lambda i, j: (0, 0)),
                  pl.BlockSpec((D, tn), lambda i, j: (0, j))],
        out_specs=pl.BlockSpec((tm, tn), lambda i, j: (i, j)),
        out_shape=jax.ShapeDtypeStruct((R, N), BF16),
        scratch_shapes=[pltpu.VMEM((tm, D), BF16)],
        compiler_params=_cparams("parallel", "arbitrary"),
        name="norm_matmul",
    )(h2, g.reshape(1, D), w)


def _matmul_residual_kernel(y_ref, w_ref, h_ref, o_ref):
    o_ref[...] = h_ref[...] + jnp.dot(y_ref[...], w_ref[...], preferred_element_type=F32)


def _matmul_residual(y, w, h2):
    R, K = y.shape
    D = w.shape[1]
    tm = _pick(R, (512, 256, 128))
    return pl.pallas_call(
        _matmul_residual_kernel,
        grid=(R // tm,),
        in_specs=[pl.BlockSpec((tm, K), lambda i: (i, 0)),
                  pl.BlockSpec((K, D), lambda i: (0, 0)),
                  pl.BlockSpec((tm, D), lambda i: (i, 0))],
        out_specs=pl.BlockSpec((tm, D), lambda i: (i, 0)),
        out_shape=jax.ShapeDtypeStruct((R, D), F32),
        input_output_aliases={2: 0},
        compiler_params=_cparams("parallel"),
        name="matmul_residual",
    )(y, w, h2)


def _retention_kernel(p_ref, cos_ref, sin_ref, dec_ref, zeta_ref, xi_ref, cd_ref, gn_ref, o_ref, state_scr):
    @pl.when(pl.program_id(1) == 0)
    def _():
        state_scr[...] = jnp.zeros_like(state_scr)

    cos = cos_ref[...]
    sin = sin_ref[...]
    k0, v0, g0 = D_MODEL, 2 * D_MODEL, 4 * D_MODEL
    heads = range(RET_HEADS)
    vcol = [slice(v0 + h * RET_DV, v0 + (h + 1) * RET_DV) for h in heads]
    qb, kb, kz = [], [], []
    for h in heads:
        q = p_ref[:, h * RET_DK:(h + 1) * RET_DK].astype(F32)
        k = p_ref[:, k0 + h * RET_DK:k0 + (h + 1) * RET_DK].astype(F32)
        qr = q * cos + pltpu.roll(q, RET_DK // 2, 1) * sin
        kr = (k * cos + pltpu.roll(k, RET_DK // 2, 1) * sin) * (RET_DK ** -0.5)
        qb.append(qr.astype(BF16))
        kb.append(kr.astype(BF16))
        kz.append((kr * zeta_ref[h]).astype(BF16))
    s = [lax.dot_general(qb[h], kb[h], (((1,), (1,)), ((), ())), preferred_element_type=F32) for h in heads]
    cross = [jnp.dot(qb[h], state_scr[h].astype(BF16), preferred_element_type=F32) for h in heads]
    kv = [lax.dot_general(kz[h], p_ref[:, vcol[h]], (((0,), (0,)), ((), ())), preferred_element_type=F32)
          for h in heads]
    sb = [(s[h] * dec_ref[h]).astype(BF16) for h in heads]
    for h in heads:
        state_scr[h] = state_scr[h] * cd_ref[h] + kv[h]
    intra = [jnp.dot(sb[h], p_ref[:, vcol[h]], preferred_element_type=F32) for h in heads]
    for h in heads:
        y = intra[h] + cross[h] * xi_ref[h]
        mu = jnp.mean(y, axis=-1, keepdims=True)
        d = y - mu
        var = jnp.mean(d * d, axis=-1, keepdims=True)
        yn = d * lax.rsqrt(var + EPS) * gn_ref[:, h * RET_DV:(h + 1) * RET_DV]
        gate = p_ref[:, g0 + h * RET_DV:g0 + (h + 1) * RET_DV].astype(F32)
        o_ref[:, h * RET_DV:(h + 1) * RET_DV] = (_silu(gate) * yn).astype(o_ref.dtype)


def _retention_tables(Lp):
    half = RET_DK // 2
    inv = ROPE_BASE ** (-jnp.arange(half, dtype=F32) / half)
    pos = (jnp.arange(Lp) - PAD).astype(F32)
    ang = pos[:, None] * inv[None, :]
    cos, sin = jnp.cos(ang), jnp.sin(ang)
    cos2 = jnp.concatenate([cos, cos], axis=-1)
    sin2 = jnp.concatenate([-sin, sin], axis=-1)
    log_g = jnp.log1p(-jnp.exp2(-5.0 - jnp.arange(RET_HEADS, dtype=F32)))
    idx = jnp.arange(BLOCK, dtype=F32)
    diff = idx[:, None] - idx[None, :]
    decay = jnp.where(diff >= 0, jnp.exp(jnp.maximum(diff, 0.0)[None] * log_g[:, None, None]), 0.0)
    zeta = jnp.exp((BLOCK - 1 - idx)[None, :] * log_g[:, None])
    xi = jnp.exp((idx + 1)[None, :] * log_g[:, None])
    cd = jnp.exp(BLOCK * log_g)
    zeta_b = jnp.broadcast_to(zeta[:, :, None], (RET_HEADS, BLOCK, RET_DK))
    xi_b = jnp.broadcast_to(xi[:, :, None], (RET_HEADS, BLOCK, RET_DV))
    cd_b = jnp.broadcast_to(cd[:, None, None], (RET_HEADS, 1, RET_DV))
    return cos2, sin2, decay, zeta_b, xi_b, cd_b


def _retention(p3, gn_g):
    B, Lp, W = p3.shape
    cos2, sin2, decay, zeta_b, xi_b, cd_b = _retention_tables(Lp)
    const3 = lambda b, c: (0, 0, 0)
    return pl.pallas_call(
        _retention_kernel,
        grid=(B, Lp // BLOCK),
        in_specs=[pl.BlockSpec((None, BLOCK, W), lambda b, c: (b, c, 0)),
                  pl.BlockSpec((BLOCK, RET_DK), lambda b, c: (c, 0)),
                  pl.BlockSpec((BLOCK, RET_DK), lambda b, c: (c, 0)),
                  pl.BlockSpec((RET_HEADS, BLOCK, BLOCK), const3),
                  pl.BlockSpec((RET_HEADS, BLOCK, RET_DK), const3),
                  pl.BlockSpec((RET_HEADS, BLOCK, RET_DV), const3),
                  pl.BlockSpec((RET_HEADS, 1, RET_DV), const3),
                  pl.BlockSpec((1, 2 * D_MODEL), lambda b, c: (0, 0))],
        out_specs=pl.BlockSpec((None, BLOCK, 2 * D_MODEL), lambda b, c: (b, c, 0)),
        out_shape=jax.ShapeDtypeStruct((B, Lp, 2 * D_MODEL), BF16),
        scratch_shapes=[pltpu.VMEM((RET_HEADS, RET_DK, RET_DV), F32)],
        compiler_params=_cparams("parallel", "arbitrary"),
        name="retention",
    )(p3, cos2, sin2, decay, zeta_b, xi_b, cd_b, gn_g.reshape(1, 2 * D_MODEL))


SB_PAIRS = D_MODEL // LANES
LOG2E = 1.4426950408889634


def _sb_kernel(q_ref, k_ref, v_ref, u_ref, o_ref, q_scr, acc_scr, car_scr):
    i = pl.program_id(1)
    lane = lax.broadcasted_iota(jnp.int32, (BLOCK, LANES), 1)
    for p in range(SB_PAIRS):
        qs = q_ref[:, p * LANES:(p + 1) * LANES] * (SB_HD ** -0.5)
        zero = jnp.zeros_like(qs)
        q_scr[p, :BLOCK] = jnp.where(lane < SB_HD, qs, zero)
        q_scr[p, BLOCK:] = jnp.where(lane >= SB_HD, qs, zero)
    row = lax.broadcasted_iota(jnp.int32, (2 * BLOCK, BLOCK), 0)
    col = lax.broadcasted_iota(jnp.int32, (2 * BLOCK, BLOCK), 1)
    qpos = i * BLOCK + (row & (BLOCK - 1))
    acc_scr[...] = jnp.zeros_like(acc_scr)
    car_scr[...] = jnp.zeros_like(car_scr)

    def key_blocks(js, masked):
        rows = [pl.ds(pl.multiple_of(j * BLOCK, BLOCK), BLOCK) for j in js]
        if masked:
            masks = [((j * BLOCK + col) < qpos) & ((j * BLOCK + col) >= PAD) for j in js]
            sel = lambda b, x: jnp.where(masks[b], x, 0.0)
        else:
            sel = lambda b, x: x
        items = [(b, p) for b in range(len(js)) for p in range(SB_PAIRS)]
        cols = [slice(p * LANES, (p + 1) * LANES) for p in range(SB_PAIRS)]
        z = {(b, p): lax.dot_general(q_scr[p], k_ref[rows[b], cols[p]], (((1,), (1,)), ((), ())),
                                     preferred_element_type=F32) for b, p in items}
        log_beta = {k: jnp.minimum(z[k], 0.0) - jnp.log(1.0 + jnp.exp2(jnp.abs(z[k]) * (-LOG2E))) for k in items}
        log_1m = {(b, p): sel(b, log_beta[b, p] - z[b, p]).astype(BF16) for b, p in items}
        r = {k: jnp.dot(log_1m[k], u_ref[...], preferred_element_type=F32) for k in items}
        after = {p: car_scr[p] for p in range(SB_PAIRS)}
        w = {}
        for b, p in items:
            w[b, p] = sel(b, jnp.exp(log_beta[b, p] + (after[p] + r[b, p][:, :BLOCK]))).astype(BF16)
            after[p] = after[p] + r[b, p][:, BLOCK:]
        o = {(b, p): jnp.dot(w[b, p], v_ref[rows[b], cols[p]], preferred_element_type=F32) for b, p in items}
        for p in range(SB_PAIRS):
            acc_scr[p] += sum(o[b, p] for b in range(len(js)))
            car_scr[p] = after[p]

    key_blocks([i], True)

    n_int = jnp.maximum(i - 1, 0)

    def interior(t, carry):
        j = i - 1 - 2 * t
        key_blocks([j, j - 1], False)
        return carry

    lax.fori_loop(0, n_int // 2, interior, 0)

    @pl.when(n_int % 2 == 1)
    def _():
        key_blocks([1], False)

    @pl.when(i > 0)
    def _():
        key_blocks([0], True)
    for p in range(SB_PAIRS):
        o_ref[:, p * LANES:(p + 1) * LANES] = jnp.where(
            lane < SB_HD, acc_scr[p, :BLOCK], acc_scr[p, BLOCK:]).astype(o_ref.dtype)


def _stick_breaking(qkv3):
    B, Lp, _ = qkv3.shape
    j_idx = np.arange(BLOCK)
    u = np.concatenate([(j_idx[:, None] > j_idx[None, :]).astype(np.float32),
                        np.ones((BLOCK, BLOCK), np.float32)], axis=1)
    return pl.pallas_call(
        _sb_kernel,
        grid=(B, Lp // BLOCK),
        in_specs=[pl.BlockSpec((None, BLOCK, D_MODEL), lambda b, i: (b, i, 0)),
                  pl.BlockSpec((None, Lp, D_MODEL), lambda b, i: (b, 0, 1)),
                  pl.BlockSpec((None, Lp, D_MODEL), lambda b, i: (b, 0, 2)),
                  pl.BlockSpec((BLOCK, 2 * BLOCK), lambda b, i: (0, 0))],
        out_specs=pl.BlockSpec((None, BLOCK, D_MODEL), lambda b, i: (b, i, 0)),
        out_shape=jax.ShapeDtypeStruct((B, Lp, D_MODEL), BF16),
        scratch_shapes=[pltpu.VMEM((SB_PAIRS, 2 * BLOCK, LANES), BF16),
                        pltpu.VMEM((SB_PAIRS, 2 * BLOCK, LANES), F32),
                        pltpu.VMEM((SB_PAIRS, 2 * BLOCK, BLOCK), F32)],
        compiler_params=_cparams("parallel", "arbitrary"),
        name="stick_breaking",
    )(qkv3, qkv3, qkv3, jnp.asarray(u, BF16))


def _pool_kernel(h_ref, g_ref, bm_ref, bh_ref, w_ref, sc_ref, o_ref, prev_scr):
    prev_scr[...] = jnp.zeros_like(prev_scr)

    def tile(j, carry):
        rows = pl.ds(pl.multiple_of(j * BLOCK, BLOCK), BLOCK)
        x = h_ref[rows, :]
        pos = j * BLOCK + lax.broadcasted_iota(jnp.int32, (BLOCK, POOL_GD), 0) - PAD
        a = _rms(x, g_ref[...])
        a_hi = a.astype(BF16)
        a_lo = (a - a_hi.astype(F32)).astype(BF16)
        for g, win in enumerate(POOL_WINDOWS):
            sl = slice(g * POOL_GD, (g + 1) * POOL_GD)
            ws = (jnp.dot(bm_ref[g], a_hi[:, sl], preferred_element_type=F32)
                  + jnp.dot(bm_ref[g], a_lo[:, sl], preferred_element_type=F32)
                  + jnp.dot(bh_ref[g], prev_scr[0, :, sl], preferred_element_type=F32)
                  + jnp.dot(bh_ref[g], prev_scr[1, :, sl], preferred_element_type=F32))
            cnt = jnp.clip(pos + 1, 1, win).astype(F32)
            diff = ws / cnt - a[:, sl]
            y = jnp.dot(diff.astype(BF16), w_ref[g], preferred_element_type=F32) * sc_ref[:, sl]
            o_ref[rows, sl] = x[:, sl] + y
        prev_scr[0] = a_hi
        prev_scr[1] = a_lo
        return carry

    lax.fori_loop(0, h_ref.shape[0] // BLOCK, tile, 0, unroll=2)


def _pool_mixer(h3, g, w_pool, scale):
    B, Lp, D = h3.shape
    t = np.arange(BLOCK)
    d_main = t[:, None] - t[None, :]
    d_halo = t[:, None] + BLOCK - t[None, :]
    bm = np.stack([((d_main >= 0) & (d_main < w)) for w in POOL_WINDOWS]).astype(np.float32)
    bh = np.stack([(d_halo < w) for w in POOL_WINDOWS]).astype(np.float32)
    ng = len(POOL_WINDOWS)
    return pl.pallas_call(
        _pool_kernel,
        grid=(B,),
        in_specs=[pl.BlockSpec((None, Lp, D), lambda b: (b, 0, 0)),
                  pl.BlockSpec((1, D), lambda b: (0, 0)),
                  pl.BlockSpec((ng, BLOCK, BLOCK), lambda b: (0, 0, 0)),
                  pl.BlockSpec((ng, BLOCK, BLOCK), lambda b: (0, 0, 0)),
                  pl.BlockSpec((ng, POOL_GD, POOL_GD), lambda b: (0, 0, 0)),
                  pl.BlockSpec((1, D), lambda b: (0, 0))],
        out_specs=pl.BlockSpec((None, Lp, D), lambda b: (b, 0, 0)),
        out_shape=jax.ShapeDtypeStruct((B, Lp, D), F32),
        scratch_shapes=[pltpu.VMEM((2, BLOCK, D), BF16)],
        input_output_aliases={0: 0},
        compiler_params=_cparams("parallel"),
        name="pool_mixer",
    )(h3, g.reshape(1, D), jnp.asarray(bm, BF16), jnp.asarray(bh, BF16), w_pool.astype(BF16),
      scale.reshape(1, D))


FFN_CHUNK = 256


def _ffn_kernel(x_ref, g_ref, w1_ref, w3_ref, w2_ref, o_ref, a_scr):
    x = x_ref[...]
    a_scr[...] = _rms(x, g_ref[...]).astype(BF16)
    for c in range(w1_ref.shape[-1] // FFN_CHUNK):
        cs = slice(c * FFN_CHUNK, (c + 1) * FFN_CHUNK)
        a = a_scr[...]
        h1 = jnp.dot(a, w1_ref[:, cs], preferred_element_type=F32)
        h3 = jnp.dot(a, w3_ref[:, cs], preferred_element_type=F32)
        y = jnp.dot((_silu(h1) * h3).astype(BF16), w2_ref[cs, :], preferred_element_type=F32)
        if c == 0:
            o_ref[...] = x + y
        else:
            o_ref[...] += y


def _ffn(x2, g, w1, w3, w2):
    R, D = x2.shape
    F = w1.shape[-1]
    tm = _pick(R, (512, 256, 128))
    return pl.pallas_call(
        _ffn_kernel,
        grid=(R // tm,),
        in_specs=[pl.BlockSpec((tm, D), lambda t: (t, 0)),
                  pl.BlockSpec((1, D), lambda t: (0, 0)),
                  pl.BlockSpec((D, F), lambda t: (0, 0)),
                  pl.BlockSpec((D, F), lambda t: (0, 0)),
                  pl.BlockSpec((F, D), lambda t: (0, 0))],
        out_specs=pl.BlockSpec((tm, D), lambda t: (t, 0)),
        out_shape=jax.ShapeDtypeStruct((R, D), F32),
        scratch_shapes=[pltpu.VMEM((tm, D), BF16)],
        input_output_aliases={0: 0},
        compiler_params=_cparams("parallel"),
        name="swiglu_dense",
    )(x2, g.reshape(1, D), w1, w3, w2)


def _router_kernel(h_ref, g_ref, wr_ref, tri_ref, slab_ref, cnt_ref, carry_scr):
    first = (pl.program_id(0) == 0) & (pl.program_id(1) == 0)

    @pl.when(first)
    def _():
        carry_scr[...] = jnp.zeros_like(carry_scr)

    tr = h_ref.shape[0]
    a = _rms(h_ref[...], g_ref[...])
    a_hi = a.astype(BF16)
    a_lo = (a - a_hi.astype(F32)).astype(BF16)
    wr = wr_ref[...]
    w_hi = wr.astype(BF16)
    w_lo = (wr - w_hi.astype(F32)).astype(BF16)
    logits = (jnp.dot(a_hi, w_hi, preferred_element_type=F32)
              + jnp.dot(a_hi, w_lo, preferred_element_type=F32)
              + jnp.dot(a_lo, w_hi, preferred_element_type=F32))
    lane = lax.broadcasted_iota(jnp.int32, (tr, LANES), 1).astype(F32)
    neg = jnp.float32(-jnp.inf)
    logits = jnp.where(lane < N_EXPERTS, logits, neg)
    m1 = jnp.max(logits, axis=-1, keepdims=True)
    i1 = jnp.min(jnp.where(logits == m1, lane, float(LANES)), axis=-1, keepdims=True)
    rest = jnp.where(lane == i1, neg, logits)
    m2 = jnp.max(rest, axis=-1, keepdims=True)
    i2 = jnp.min(jnp.where(rest == m2, lane, float(LANES)), axis=-1, keepdims=True)
    e = jnp.exp(m2 - m1)
    g1 = 1.0 / (1.0 + e)
    g2 = e / (1.0 + e)
    pos = pl.program_id(1) * tr + lax.broadcasted_iota(jnp.int32, (tr, LANES), 0)
    valid = (pos >= PAD).astype(F32)
    oh1 = (lane == i1).astype(F32)
    oh2 = (lane == i2).astype(F32)
    chosen = (oh1 + oh2) * valid
    before = carry_scr[0:1, :] + jnp.dot(tri_ref[...], chosen.astype(BF16), preferred_element_type=F32)
    r1 = jnp.sum(before * oh1, axis=-1, keepdims=True)
    r2 = jnp.sum(before * oh2, axis=-1, keepdims=True)
    total = carry_scr[0:1, :] + jnp.sum(chosen, axis=0, keepdims=True)
    carry_scr[...] = jnp.broadcast_to(total, carry_scr.shape)
    cnt_ref[...] = jnp.broadcast_to(total, cnt_ref.shape)
    cols = (i1, i2, g1 * valid, g2 * valid, r1, r2)
    slab = jnp.zeros((tr, LANES), F32)
    for c, val in enumerate(cols):
        slab = jnp.where(lane == c, val, slab)
    slab_ref[...] = slab


def _router(h3, g, w_router):
    B, Lp, D = h3.shape
    tr = Lp // 2
    wr = jnp.zeros((D, LANES), F32).at[:, :N_EXPERTS].set(w_router)
    t = np.arange(tr)
    tri = (t[:, None] > t[None, :]).astype(np.float32)
    slab, cnt = pl.pallas_call(
        _router_kernel,
        grid=(B, Lp // tr),
        in_specs=[pl.BlockSpec((None, tr, D), lambda b, j: (b, j, 0)),
                  pl.BlockSpec((1, D), lambda b, j: (0, 0)),
                  pl.BlockSpec((D, LANES), lambda b, j: (0, 0)),
                  pl.BlockSpec((tr, tr), lambda b, j: (0, 0))],
        out_specs=[pl.BlockSpec((tr, LANES), lambda b, j: (b * (Lp // tr) + j, 0)),
                   pl.BlockSpec((8, LANES), lambda b, j: (0, 0))],
        out_shape=[jax.ShapeDtypeStruct((B * Lp, LANES), F32), jax.ShapeDtypeStruct((8, LANES), F32)],
        scratch_shapes=[pltpu.VMEM((8, LANES), F32)],
        compiler_params=_cparams("arbitrary", "arbitrary"),
        name="moe_router",
    )(h3, g.reshape(1, D), wr, jnp.asarray(tri, BF16))
    return slab, cnt[0, :N_EXPERTS]


def _invert_kernel(lo_ref, hi_ref, spare_ref, p1_hbm, p2_hbm, inv_ref, buf1, buf2, sem, *, n_tok):
    chunk = buf1.shape[0]
    tile_mask = spare_ref.shape[0] - 1

    def fill(r, carry):
        inv_ref[r] = spare_ref[r & tile_mask]
        return carry

    for e in range(N_EXPERTS + 1):
        lax.fori_loop(lo_ref[e], hi_ref[e], fill, 0)
    for c in range(n_tok // chunk):
        c1 = pltpu.make_async_copy(p1_hbm.at[pl.ds(c * chunk, chunk)], buf1, sem.at[0])
        c2 = pltpu.make_async_copy(p2_hbm.at[pl.ds(c * chunk, chunk)], buf2, sem.at[1])
        c1.start()
        c2.start()
        c1.wait()
        c2.wait()

        def token(i, carry, c=c):
            inv_ref[buf1[i]] = c * chunk + i
            inv_ref[buf2[i]] = n_tok + c * chunk + i
            return carry

        lax.fori_loop(0, chunk, token, 0, unroll=8)


def _invert(p1, p2, pad_lo, pad_hi, spare, n_rows):
    n_tok = p1.shape[0]
    chunk = _pick(n_tok, (2048, 1024, 512, 256, 128))
    grid_spec = pltpu.PrefetchScalarGridSpec(
        num_scalar_prefetch=3,
        grid=(1,),
        in_specs=[pl.BlockSpec(memory_space=pl.ANY), pl.BlockSpec(memory_space=pl.ANY)],
        out_specs=pl.BlockSpec(memory_space=pltpu.SMEM),
        scratch_shapes=[pltpu.SMEM((chunk,), jnp.int32), pltpu.SMEM((chunk,), jnp.int32),
                        pltpu.SemaphoreType.DMA((2,))])
    return pl.pallas_call(
        functools.partial(_invert_kernel, n_tok=n_tok),
        grid_spec=grid_spec,
        out_shape=jax.ShapeDtypeStruct((n_rows + 1,), jnp.int32),
        compiler_params=_cparams("arbitrary"),
        name="moe_invert",
    )(pad_lo, pad_hi, spare, p1, p2)


MOE_TILE = 512
MOE_GATES = 4


def _moe_ffn_kernel(te_ref, inv_ref, spare_ref, zero_ref, h_hbm, g_ref, w1_ref, w3_ref, w2_ref, y_hbm,
                    xbuf, ybuf, a_scr, gsem, ssem, *, n_tok, n_batch, seq_pad):
    del te_ref
    t = pl.program_id(0)
    n = pl.num_programs(0)
    tm = xbuf.shape[0]
    slot = t % 2
    other = 1 - slot

    def gather_row(tile, r, off=0):
        idx = inv_ref[tile * tm + r + off]
        src = jnp.where(idx >= n_tok, idx - n_tok, idx)
        pltpu.make_async_copy(h_hbm.at[pl.ds(src, 1)], xbuf.at[pl.ds(r, 1)], gsem.at[0]).start()

    def scatter_row(tile, buf_slot, r, off=0):
        dst = jnp.where(tile >= 0, inv_ref[jnp.maximum(tile, 0) * tm + r + off], spare_ref[r])
        pltpu.make_async_copy(ybuf.at[buf_slot, pl.ds(r, 1)], y_hbm.at[pl.ds(dst, 1)], ssem.at[buf_slot]).start()

    def wait_gather():
        pltpu.make_async_copy(h_hbm.at[pl.ds(0, tm)], xbuf, gsem.at[0]).wait()

    def wait_scatter(buf_slot):
        pltpu.make_async_copy(ybuf.at[buf_slot], y_hbm.at[pl.ds(0, tm)], ssem.at[buf_slot]).wait()

    @pl.when(t == 0)
    def _():
        ybuf[...] = jnp.zeros_like(ybuf)
        zero_copies = [pltpu.make_async_copy(ybuf.at[0, pl.ds(0, PAD)],
                                             y_hbm.at[pl.ds(k * n_tok + b * seq_pad, PAD)], ssem.at[0])
                       for k in range(2) for b in range(n_batch)]
        for cp in zero_copies:
            cp.start()
        for cp in zero_copies:
            cp.wait()

        def first(r, carry):
            gather_row(0, r)
            return carry

        lax.fori_loop(0, tm, first, 0, unroll=8)

    @pl.when(t > 0)
    def _():
        wait_scatter(slot)

    wait_gather()
    a_scr[...] = _rms(xbuf[...], g_ref[...]).astype(BF16)
    nxt = jnp.minimum(t + 1, n - 1)
    prv = t - 1
    n_chunks = w1_ref.shape[-1] // FFN_CHUNK
    bounds = [(c * tm) // n_chunks for c in range(n_chunks + 1)]
    for c in range(n_chunks):
        cs = slice(c * FFN_CHUNK, (c + 1) * FFN_CHUNK)
        a = a_scr[...]
        h1 = jnp.dot(a, w1_ref[0, :, cs], preferred_element_type=F32)
        h3 = jnp.dot(a, w3_ref[0, :, cs], preferred_element_type=F32)
        y = jnp.dot((_silu(h1) * h3).astype(BF16), w2_ref[0, cs, :], preferred_element_type=F32)
        if c == 0:
            ybuf[slot] = y
        else:
            ybuf[slot] += y
        gates = [res[rb * (tm // MOE_GATES):rb * (tm // MOE_GATES) + 1, 0:1]
                 for res in (h1, h3, y) for rb in range(MOE_GATES)]
        rows = list(range(bounds[c], bounds[c + 1]))
        for gi, gate in enumerate(gates):
            off = gate[0, 0].astype(jnp.int32) * zero_ref[0]
            for r in rows[(gi * len(rows)) // len(gates):((gi + 1) * len(rows)) // len(gates)]:
                gather_row(nxt, r, off)
                scatter_row(prv, other, r, off)

    @pl.when(t == n - 1)
    def _():
        def last(r, carry):
            scatter_row(t, slot, r)
            return carry

        wait_scatter(other)
        lax.fori_loop(0, tm, last, 0, unroll=8)
        wait_gather()
        wait_scatter(slot)


def _moe_ffn(h2, g, w1, w3, w2, tile_expert, inv, spare, n_rows, n_batch):
    R, D = h2.shape
    F = w1.shape[-1]
    tm = MOE_TILE
    kern = functools.partial(_moe_ffn_kernel, n_tok=R, n_batch=n_batch, seq_pad=R // n_batch)
    grid_spec = pltpu.PrefetchScalarGridSpec(
        num_scalar_prefetch=4,
        grid=(n_rows // tm,),
        in_specs=[pl.BlockSpec(memory_space=pl.ANY),
                  pl.BlockSpec((1, D), lambda t, te, inv, sp, z: (0, 0)),
                  pl.BlockSpec((1, D, F), lambda t, te, inv, sp, z: (te[t], 0, 0)),
                  pl.BlockSpec((1, D, F), lambda t, te, inv, sp, z: (te[t], 0, 0)),
                  pl.BlockSpec((1, F, D), lambda t, te, inv, sp, z: (te[t], 0, 0))],
        out_specs=pl.BlockSpec(memory_space=pl.ANY),
        scratch_shapes=[pltpu.VMEM((tm, D), F32), pltpu.VMEM((2, tm, D), F32), pltpu.VMEM((tm, D), BF16),
                        pltpu.SemaphoreType.DMA((1,)), pltpu.SemaphoreType.DMA((2,))])
    return pl.pallas_call(
        kern,
        grid_spec=grid_spec,
        out_shape=jax.ShapeDtypeStruct((2 * R, D), F32),
        compiler_params=_cparams("arbitrary"),
        name="swiglu_experts",
    )(tile_expert, inv, spare, jnp.zeros((1,), jnp.int32), h2, g.reshape(1, D), w1, w3, w2)


def _combine_kernel(h_ref, slab_ref, y1_ref, y2_ref, o_ref):
    slab = slab_ref[...]
    o_ref[...] = h_ref[...] + (slab[:, 2:3] * y1_ref[...] + slab[:, 3:4] * y2_ref[...])


def _combine(h2, slab, y):
    R, D = h2.shape
    tc = _pick(R, (512, 256, 128))
    return pl.pallas_call(
        _combine_kernel,
        grid=(R // tc,),
        in_specs=[pl.BlockSpec((tc, D), lambda i: (i, 0)),
                  pl.BlockSpec((tc, LANES), lambda i: (i, 0)),
                  pl.BlockSpec((tc, D), lambda i: (i, 0)),
                  pl.BlockSpec((tc, D), lambda i: (i + R // tc, 0))],
        out_specs=pl.BlockSpec((tc, D), lambda i: (i, 0)),
        out_shape=jax.ShapeDtypeStruct((R, D), F32),
        input_output_aliases={0: 0},
        compiler_params=_cparams("parallel"),
        name="moe_combine",
    )(h2, slab, y, y)


def _moe(h3, g, w_router, w1, w3, w2):
    B, Lp, D = h3.shape
    tm = MOE_TILE
    n_tokens = B * (Lp - PAD)
    n_tiles = (2 * n_tokens) // tm + N_EXPERTS
    n_rows = n_tiles * tm
    slab, counts = _router(h3, g, w_router)
    counts = counts.astype(jnp.int32)
    tiles_e = (counts + tm - 1) // tm
    end_tile = jnp.cumsum(tiles_e)
    offset = ((end_tile - tiles_e) * tm).astype(F32)
    n_used = end_tile[-1:]
    expert_ids = jnp.arange(N_EXPERTS, dtype=F32)
    e1, e2 = slab[:, 0:1], slab[:, 1:2]
    valid = (jnp.arange(B * Lp) % Lp) >= PAD
    off1 = jnp.sum(jnp.where(e1 == expert_ids[None, :], offset[None, :], 0.0), axis=-1)
    off2 = jnp.sum(jnp.where(e2 == expert_ids[None, :], offset[None, :], 0.0), axis=-1)
    p1 = jnp.where(valid, off1 + slab[:, 4], float(n_rows)).astype(jnp.int32)
    p2 = jnp.where(valid, off2 + slab[:, 5], float(n_rows)).astype(jnp.int32)
    tile_ids = jnp.arange(n_tiles, dtype=jnp.int32)
    last_tile = jnp.maximum(n_used - 1, 0)
    tile_expert = jnp.sum(jnp.minimum(tile_ids, last_tile)[:, None] >= end_tile[None, :], axis=-1).astype(jnp.int32)
    tile_expert = jnp.minimum(tile_expert, N_EXPERTS - 1)
    group_start = (end_tile - tiles_e) * tm
    pad_lo = jnp.concatenate([group_start + counts, n_used * tm]).astype(jnp.int32)
    pad_hi = jnp.concatenate([end_tile * tm, jnp.full((1,), n_rows + 1, jnp.int32)]).astype(jnp.int32)
    spare_rows = [k * B * Lp + b * Lp + o for k in range(2) for b in range(B) for o in range(PAD)]
    assert tm & (tm - 1) == 0 and len(spare_rows) >= tm
    spare = jnp.asarray(np.array(spare_rows[:tm], np.int32))
    inv = _invert(p1, p2, pad_lo, pad_hi, spare, n_rows)
    h2 = h3.reshape(B * Lp, D)
    ys = _moe_ffn(h2, g, w1, w3, w2, tile_expert, inv, spare, n_rows, B)
    return _combine(h2, slab, ys).reshape(B, Lp, D)


def _final_norm_kernel(h_ref, g_ref, o_ref):
    lead = h_ref.shape[0] - o_ref.shape[0]
    o_ref[...] = _rms(h_ref[lead:, :], g_ref[...])


def _final_norm(h3, g):
    B, Lp, D = h3.shape
    S = Lp - PAD - N_META
    return pl.pallas_call(
        _final_norm_kernel,
        grid=(B,),
        in_specs=[pl.BlockSpec((None, Lp, D), lambda b: (b, 0, 0)),
                  pl.BlockSpec((1, D), lambda b: (0, 0))],
        out_specs=pl.BlockSpec((None, S, D), lambda b: (b, 0, 0)),
        out_shape=jax.ShapeDtypeStruct((B, S, D), F32),
        compiler_params=_cparams("parallel"),
        name="final_norm",
    )(h3, g.reshape(1, D))


def kernel(x, meta_tokens, norm_mix_g, norm_ffn_g, ret_wq, ret_wk, ret_wv, ret_wg, ret_wo, ret_gn_g,
           sb_wqkv, sb_wo, pool_w, pool_scale, ffn_w1, ffn_w3, ffn_w2,
           moe_router, moe_w1, moe_w3, moe_w2, final_norm_g):
    B, S, D = x.shape
    Lp = PAD + N_META + S
    R = B * Lp
    depth = norm_mix_g.shape[0]
    meta = jnp.broadcast_to(meta_tokens.astype(x.dtype)[None], (B, N_META, D))
    h = jnp.concatenate([jnp.zeros((B, PAD, D), x.dtype), meta, x], axis=1)
    for i in range(depth):
        m, j = i % 3, i // 3
        if m == 0:
            w_in = jnp.concatenate([ret_wq[j], ret_wk[j], ret_wv[j], ret_wg[j]], axis=1).astype(BF16)
            p = _norm_matmul(h.reshape(R, D), norm_mix_g[i], w_in)
            y = _retention(p.reshape(B, Lp, -1), ret_gn_g[j])
            h = _matmul_residual(y.reshape(R, -1), ret_wo[j].astype(BF16), h.reshape(R, D)).reshape(B, Lp, D)
        elif m == 1:
            qkv = _norm_matmul(h.reshape(R, D), norm_mix_g[i], sb_wqkv[j].astype(BF16))
            y = _stick_breaking(qkv.reshape(B, Lp, -1))
            h = _matmul_residual(y.reshape(R, D), sb_wo[j].astype(BF16), h.reshape(R, D)).reshape(B, Lp, D)
        else:
            h = _pool_mixer(h, norm_mix_g[i], pool_w[j], pool_scale[j])
        c = i // 2
        if i % 2 == 0:
            h = _ffn(h.reshape(R, D), norm_ffn_g[i], ffn_w1[c].astype(BF16), ffn_w3[c].astype(BF16),
                     ffn_w2[c].astype(BF16)).reshape(B, Lp, D)
        else:
            h = _moe(h, norm_ffn_g[i], moe_router[c], moe_w1[c].astype(BF16), moe_w3[c].astype(BF16),
                     moe_w2[c].astype(BF16))
    return _final_norm(h, final_norm_g)
```

```python
import functools

import numpy as np
import jax
import jax.numpy as jnp
from jax import lax
from jax.experimental import pallas as pl
from jax.experimental.pallas import tpu as pltpu

F32 = jnp.float32
BF16 = jnp.bfloat16

D_MODEL = 1024
N_META = 16
BLOCK = 128
PAD = BLOCK - N_META
EPS = 1e-6
RET_HEADS = 8
RET_DK = D_MODEL // RET_HEADS
RET_DV = 2 * D_MODEL // RET_HEADS
ROPE_BASE = 10000.0
SB_HEADS = 16
SB_HD = D_MODEL // SB_HEADS
POOL_WINDOWS = (2, 4, 8, 16)
POOL_GD = D_MODEL // len(POOL_WINDOWS)
N_EXPERTS = 8
LANES = 128
VMEM_LIMIT_BYTES = 58 * 1024 * 1024


def _cparams(*sem):
    return pltpu.CompilerParams(dimension_semantics=sem, vmem_limit_bytes=VMEM_LIMIT_BYTES)


def _pick(n, cands):
    for c in cands:
        if n % c == 0:
            return c
    raise ValueError(f"no tile for {n} in {cands}")


def _rms(x, g):
    ms = jnp.mean(x * x, axis=-1, keepdims=True)
    return x * lax.rsqrt(ms + EPS) * g


def _silu(x):
    return x * (1.0 / (1.0 + jnp.exp(-x)))


def _norm_matmul_kernel(h_ref, g_ref, w_ref, o_ref, a_scr):
    @pl.when(pl.program_id(1) == 0)
    def _():
        a_scr[...] = _rms(h_ref[...], g_ref[...]).astype(BF16)

    o_ref[...] = jnp.dot(a_scr[...], w_ref[...], preferred_element_type=F32).astype(o_ref.dtype)


def _norm_matmul(h2, g, w):
    R, D = h2.shape
    N = w.shape[1]
    tm = _pick(R, (1024, 512, 256, 128))
    tn = _pick(N, (2048, 1536, 1024, 512, 256, 128))
    return pl.pallas_call(
        _norm_matmul_kernel,
        grid=(R // tm, N // tn),
        in_specs=[pl.BlockSpec((tm, D), lambda i, j: (i, 0)),
                  pl.BlockSpec((1, D), lambda i, j: (0, 0)),
                  pl.BlockSpec((D, tn), lambda i, j: (0, j))],
        out_specs=pl.BlockSpec((tm, tn), lambda i, j: (i, j)),
        out_shape=jax.ShapeDtypeStruct((R, N), BF16),
        scratch_shapes=[pltpu.VMEM((tm, D), BF16)],
        compiler_params=_cparams("parallel", "arbitrary"),
        name="norm_matmul",
    )(h2, g.reshape(1, D), w)


def _matmul_residual_kernel(y_ref, w_ref, h_ref, o_ref):
    o_ref[...] = h_ref[...] + jnp.dot(y_ref[...], w_ref[...], preferred_element_type=F32)


def _matmul_residual(y, w, h2):
    R, K = y.shape
    D = w.shape[1]
    tm = _pick(R, (512, 256, 128))
    return pl.pallas_call(
        _matmul_residual_kernel,
        grid=(R // tm,),
        in_specs=[pl.BlockSpec((tm, K), lambda i: (i, 0)),
                  pl.BlockSpec((K, D), lambda i: (0, 0)),
                  pl.BlockSpec((tm, D), lambda i: (i, 0))],
        out_specs=pl.BlockSpec((tm, D), lambda i: (i, 0)),
        out_shape=jax.ShapeDtypeStruct((R, D), F32),
        input_output_aliases={2: 0},
        compiler_params=_cparams("parallel"),
        name="matmul_residual",
    )(y, w, h2)


def _retention_kernel(p_ref, cos_ref, sin_ref, dec_ref, zeta_ref, xi_ref, cd_ref, gn_ref, o_ref, state_scr):
    @pl.when(pl.program_id(1) == 0)
    def _():
        state_scr[...] = jnp.zeros_like(state_scr)

    cos = cos_ref[...]
    sin = sin_ref[...]
    k0, v0, g0 = D_MODEL, 2 * D_MODEL, 4 * D_MODEL
    heads = range(RET_HEADS)
    vcol = [slice(v0 + h * RET_DV, v0 + (h + 1) * RET_DV) for h in heads]
    qb, kb, kz = [], [], []
    for h in heads:
        q = p_ref[:, h * RET_DK:(h + 1) * RET_DK].astype(F32)
        k = p_ref[:, k0 + h * RET_DK:k0 + (h + 1) * RET_DK].astype(F32)
        qr = q * cos + pltpu.roll(q, RET_DK // 2, 1) * sin
        kr = (k * cos + pltpu.roll(k, RET_DK // 2, 1) * sin) * (RET_DK ** -0.5)
        qb.append(qr.astype(BF16))
        kb.append(kr.astype(BF16))
        kz.append((kr * zeta_ref[h]).astype(BF16))
    s = [lax.dot_general(qb[h], kb[h], (((1,), (1,)), ((), ())), preferred_element_type=F32) for h in heads]
    cross = [jnp.dot(qb[h], state_scr[h].astype(BF16), preferred_element_type=F32) for h in heads]
    kv = [lax.dot_general(kz[h], p_ref[:, vcol[h]], (((0,), (0,)), ((), ())), preferred_element_type=F32)
          for h in heads]
    sb = [(s[h] * dec_ref[h]).astype(BF16) for h in heads]
    for h in heads:
        state_scr[h] = state_scr[h] * cd_ref[h] + kv[h]
    intra = [jnp.dot(sb[h], p_ref[:, vcol[h]], preferred_element_type=F32) for h in heads]
    for h in heads:
        y = intra[h] + cross[h] * xi_ref[h]
        mu = jnp.mean(y, axis=-1, keepdims=True)
        d = y - mu
        var = jnp.mean(d * d, axis=-1, keepdims=True)
        yn = d * lax.rsqrt(var + EPS) * gn_ref[:, h * RET_DV:(h + 1) * RET_DV]
        gate = p_ref[:, g0 + h * RET_DV:g0 + (h + 1) * RET_DV].astype(F32)
        o_ref[:, h * RET_DV:(h + 1) * RET_DV] = (_silu(gate) * yn).astype(o_ref.dtype)


def _retention_tables(Lp):
    half = RET_DK // 2
    inv = ROPE_BASE ** (-jnp.arange(half, dtype=F32) / half)
    pos = (jnp.arange(Lp) - PAD).astype(F32)
    ang = pos[:, None] * inv[None, :]
    cos, sin = jnp.cos(ang), jnp.sin(ang)
    cos2 = jnp.concatenate([cos, cos], axis=-1)
    sin2 = jnp.concatenate([-sin, sin], axis=-1)
    log_g = jnp.log1p(-jnp.exp2(-5.0 - jnp.arange(RET_HEADS, dtype=F32)))
    idx = jnp.arange(BLOCK, dtype=F32)
    diff = idx[:, None] - idx[None, :]
    decay = jnp.where(diff >= 0, jnp.exp(jnp.maximum(diff, 0.0)[None] * log_g[:, None, None]), 0.0)
    zeta = jnp.exp((BLOCK - 1 - idx)[None, :] * log_g[:, None])
    xi = jnp.exp((idx + 1)[None, :] * log_g[:, None])
    cd = jnp.exp(BLOCK * log_g)
    zeta_b = jnp.broadcast_to(zeta[:, :, None], (RET_HEADS, BLOCK, RET_DK))
    xi_b = jnp.broadcast_to(xi[:, :, None], (RET_HEADS, BLOCK, RET_DV))
    cd_b = jnp.broadcast_to(cd[:, None, None], (RET_HEADS, 1, RET_DV))
    return cos2, sin2, decay, zeta_b, xi_b, cd_b


def _retention(p3, gn_g):
    B, Lp, W = p3.shape
    cos2, sin2, decay, zeta_b, xi_b, cd_b = _retention_tables(Lp)
    const3 = lambda b, c: (0, 0, 0)
    return pl.pallas_call(
        _retention_kernel,
        grid=(B, Lp // BLOCK),
        in_specs=[pl.BlockSpec((None, BLOCK, W), lambda b, c: (b, c, 0)),
                  pl.BlockSpec((BLOCK, RET_DK), lambda b, c: (c, 0)),
                  pl.BlockSpec((BLOCK, RET_DK), lambda b, c: (c, 0)),
                  pl.BlockSpec((RET_HEADS, BLOCK, BLOCK), const3),
                  pl.BlockSpec((RET_HEADS, BLOCK, RET_DK), const3),
                  pl.BlockSpec((RET_HEADS, BLOCK, RET_DV), const3),
                  pl.BlockSpec((RET_HEADS, 1, RET_DV), const3),
                  pl.BlockSpec((1, 2 * D_MODEL), lambda b, c: (0, 0))],
        out_specs=pl.BlockSpec((None, BLOCK, 2 * D_MODEL), lambda b, c: (b, c, 0)),
        out_shape=jax.ShapeDtypeStruct((B, Lp, 2 * D_MODEL), BF16),
        scratch_shapes=[pltpu.VMEM((RET_HEADS, RET_DK, RET_DV), F32)],
        compiler_params=_cparams("parallel", "arbitrary"),
        name="retention",
    )(p3, cos2, sin2, decay, zeta_b, xi_b, cd_b, gn_g.reshape(1, 2 * D_MODEL))


SB_PAIRS = D_MODEL // LANES
LOG2E = 1.4426950408889634


def _sb_kernel(q_ref, k_ref, v_ref, u_ref, o_ref, q_scr, acc_scr, car_scr):
    i = pl.program_id(1)
    lane = lax.broadcasted_iota(jnp.int32, (BLOCK, LANES), 1)
    for p in range(SB_PAIRS):
        qs = q_ref[:, p * LANES:(p + 1) * LANES] * (SB_HD ** -0.5)
        zero = jnp.zeros_like(qs)
        q_scr[p, :BLOCK] = jnp.where(lane < SB_HD, qs, zero)
        q_scr[p, BLOCK:] = jnp.where(lane >= SB_HD, qs, zero)
    row = lax.broadcasted_iota(jnp.int32, (2 * BLOCK, BLOCK), 0)
    col = lax.broadcasted_iota(jnp.int32, (2 * BLOCK, BLOCK), 1)
    qpos = i * BLOCK + (row & (BLOCK - 1))
    acc_scr[...] = jnp.zeros_like(acc_scr)
    car_scr[...] = jnp.zeros_like(car_scr)

    def key_blocks(js, masked):
        rows = [pl.ds(pl.multiple_of(j * BLOCK, BLOCK), BLOCK) for j in js]
        if masked:
            masks = [((j * BLOCK + col) < qpos) & ((j * BLOCK + col) >= PAD) for j in js]
            sel = lambda b, x: jnp.where(masks[b], x, 0.0)
        else:
            sel = lambda b, x: x
        items = [(b, p) for b in range(len(js)) for p in range(SB_PAIRS)]
        cols = [slice(p * LANES, (p + 1) * LANES) for p in range(SB_PAIRS)]
        z = {(b, p): lax.dot_general(q_scr[p], k_ref[rows[b], cols[p]], (((1,), (1,)), ((), ())),
                                     preferred_element_type=F32) for b, p in items}
        log_beta = {k: jnp.minimum(z[k], 0.0) - jnp.log(1.0 + jnp.exp2(jnp.abs(z[k]) * (-LOG2E))) for k in items}
        log_1m = {(b, p): sel(b, log_beta[b, p] - z[b, p]).astype(BF16) for b, p in items}
        r = {k: jnp.dot(log_1m[k], u_ref[...], preferred_element_type=F32) for k in items}
        after = {p: car_scr[p] for p in range(SB_PAIRS)}
        w = {}
        for b, p in items:
            w[b, p] = sel(b, jnp.exp(log_beta[b, p] + (after[p] + r[b, p][:, :BLOCK]))).astype(BF16)
            after[p] = after[p] + r[b, p][:, BLOCK:]
        o = {(b, p): jnp.dot(w[b, p], v_ref[rows[b], cols[p]], preferred_element_type=F32) for b, p in items}
        for p in range(SB_PAIRS):
            acc_scr[p] += sum(o[b, p] for b in range(len(js)))
            car_scr[p] = after[p]

    key_blocks([i], True)

    n_int = jnp.maximum(i - 1, 0)

    def interior(t, carry):
        j = i - 1 - 2 * t
        key_blocks([j, j - 1], False)
        return carry

    lax.fori_loop(0, n_int // 2, interior, 0)

    @pl.when(n_int % 2 == 1)
    def _():
        key_blocks([1], False)

    @pl.when(i > 0)
    def _():
        key_blocks([0], True)
    for p in range(SB_PAIRS):
        o_ref[:, p * LANES:(p + 1) * LANES] = jnp.where(
            lane < SB_HD, acc_scr[p, :BLOCK], acc_scr[p, BLOCK:]).astype(o_ref.dtype)


def _stick_breaking(qkv3):
    B, Lp, _ = qkv3.shape
    j_idx = np.arange(BLOCK)
    u = np.concatenate([(j_idx[:, None] > j_idx[None, :]).astype(np.float32),
                        np.ones((BLOCK, BLOCK), np.float32)], axis=1)
    return pl.pallas_call(
        _sb_kernel,
        grid=(B, Lp // BLOCK),
        in_specs=[pl.BlockSpec((None, BLOCK, D_MODEL), lambda b, i: (b, i, 0)),
                  pl.BlockSpec((None, Lp, D_MODEL), lambda b, i: (b, 0, 1)),
                  pl.BlockSpec((None, Lp, D_MODEL), lambda b, i: (b, 0, 2)),
                  pl.BlockSpec((BLOCK, 2 * BLOCK), lambda b, i: (0, 0))],
        out_specs=pl.BlockSpec((None, BLOCK, D_MODEL), lambda b, i: (b, i, 0)),
        out_shape=jax.ShapeDtypeStruct((B, Lp, D_MODEL), BF16),
        scratch_shapes=[pltpu.VMEM((SB_PAIRS, 2 * BLOCK, LANES), BF16),
                        pltpu.VMEM((SB_PAIRS, 2 * BLOCK, LANES), F32),
                        pltpu.VMEM((SB_PAIRS, 2 * BLOCK, BLOCK), F32)],
        compiler_params=_cparams("parallel", "arbitrary"),
        name="stick_breaking",
    )(qkv3, qkv3, qkv3, jnp.asarray(u, BF16))


def _pool_kernel(h_ref, g_ref, bm_ref, bh_ref, w_ref, sc_ref, o_ref, prev_scr):
    prev_scr[...] = jnp.zeros_like(prev_scr)

    def tile(j, carry):
        rows = pl.ds(pl.multiple_of(j * BLOCK, BLOCK), BLOCK)
        x = h_ref[rows, :]
        pos = j * BLOCK + lax.broadcasted_iota(jnp.int32, (BLOCK, POOL_GD), 0) - PAD
        a = _rms(x, g_ref[...])
        a_hi = a.astype(BF16)
        a_lo = (a - a_hi.astype(F32)).astype(BF16)
        for g, win in enumerate(POOL_WINDOWS):
            sl = slice(g * POOL_GD, (g + 1) * POOL_GD)
            ws = (jnp.dot(bm_ref[g], a_hi[:, sl], preferred_element_type=F32)
                  + jnp.dot(bm_ref[g], a_lo[:, sl], preferred_element_type=F32)
                  + jnp.dot(bh_ref[g], prev_scr[0, :, sl], preferred_element_type=F32)
                  + jnp.dot(bh_ref[g], prev_scr[1, :, sl], preferred_element_type=F32))
            cnt = jnp.clip(pos + 1, 1, win).astype(F32)
            diff = ws / cnt - a[:, sl]
            y = jnp.dot(diff.astype(BF16), w_ref[g], preferred_element_type=F32) * sc_ref[:, sl]
            o_ref[rows, sl] = x[:, sl] + y
        prev_scr[0] = a_hi
        prev_scr[1] = a_lo
        return carry

    lax.fori_loop(0, h_ref.shape[0] // BLOCK, tile, 0, unroll=2)


def _pool_mixer(h3, g, w_pool, scale):
    B, Lp, D = h3.shape
    t = np.arange(BLOCK)
    d_main = t[:, None] - t[None, :]
    d_halo = t[:, None] + BLOCK - t[None, :]
    bm = np.stack([((d_main >= 0) & (d_main < w)) for w in POOL_WINDOWS]).astype(np.float32)
    bh = np.stack([(d_halo < w) for w in POOL_WINDOWS]).astype(np.float32)
    ng = len(POOL_WINDOWS)
    return pl.pallas_call(
        _pool_kernel,
        grid=(B,),
        in_specs=[pl.BlockSpec((None, Lp, D), lambda b: (b, 0, 0)),
                  pl.BlockSpec((1, D), lambda b: (0, 0)),
                  pl.BlockSpec((ng, BLOCK, BLOCK), lambda b: (0, 0, 0)),
                  pl.BlockSpec((ng, BLOCK, BLOCK), lambda b: (0, 0, 0)),
                  pl.BlockSpec((ng, POOL_GD, POOL_GD), lambda b: (0, 0, 0)),
                  pl.BlockSpec((1, D), lambda b: (0, 0))],
        out_specs=pl.BlockSpec((None, Lp, D), lambda b: (b, 0, 0)),
        out_shape=jax.ShapeDtypeStruct((B, Lp, D), F32),
        scratch_shapes=[pltpu.VMEM((2, BLOCK, D), BF16)],
        input_output_aliases={0: 0},
        compiler_params=_cparams("parallel"),
        name="pool_mixer",
    )(h3, g.reshape(1, D), jnp.asarray(bm, BF16), jnp.asarray(bh, BF16), w_pool.astype(BF16),
      scale.reshape(1, D))


FFN_CHUNK = 256


def _ffn_kernel(x_ref, g_ref, w1_ref, w3_ref, w2_ref, o_ref, a_scr):
    x = x_ref[...]
    a_scr[...] = _rms(x, g_ref[...]).astype(BF16)
    for c in range(w1_ref.shape[-1] // FFN_CHUNK):
        cs = slice(c * FFN_CHUNK, (c + 1) * FFN_CHUNK)
        a = a_scr[...]
        h1 = jnp.dot(a, w1_ref[:, cs], preferred_element_type=F32)
        h3 = jnp.dot(a, w3_ref[:, cs], preferred_element_type=F32)
        y = jnp.dot((_silu(h1) * h3).astype(BF16), w2_ref[cs, :], preferred_element_type=F32)
        if c == 0:
            o_ref[...] = x + y
        else:
            o_ref[...] += y


def _ffn(x2, g, w1, w3, w2):
    R, D = x2.shape
    F = w1.shape[-1]
    tm = _pick(R, (512, 256, 128))
    return pl.pallas_call(
        _ffn_kernel,
        grid=(R // tm,),
        in_specs=[pl.BlockSpec((tm, D), lambda t: (t, 0)),
                  pl.BlockSpec((1, D), lambda t: (0, 0)),
                  pl.BlockSpec((D, F), lambda t: (0, 0)),
                  pl.BlockSpec((D, F), lambda t: (0, 0)),
                  pl.BlockSpec((F, D), lambda t: (0, 0))],
        out_specs=pl.BlockSpec((tm, D), lambda t: (t, 0)),
        out_shape=jax.ShapeDtypeStruct((R, D), F32),
        scratch_shapes=[pltpu.VMEM((tm, D), BF16)],
        input_output_aliases={0: 0},
        compiler_params=_cparams("parallel"),
        name="swiglu_dense",
    )(x2, g.reshape(1, D), w1, w3, w2)


def _router_kernel(h_ref, g_ref, wr_ref, tri_ref, slab_ref, cnt_ref, carry_scr):
    first = (pl.program_id(0) == 0) & (pl.program_id(1) == 0)

    @pl.when(first)
    def _():
        carry_scr[...] = jnp.zeros_like(carry_scr)

    tr = h_ref.shape[0]
    a = _rms(h_ref[...], g_ref[...])
    a_hi = a.astype(BF16)
    a_lo = (a - a_hi.astype(F32)).astype(BF16)
    wr = wr_ref[...]
    w_hi = wr.astype(BF16)
    w_lo = (wr - w_hi.astype(F32)).astype(BF16)
    logits = (jnp.dot(a_hi, w_hi, preferred_element_type=F32)
              + jnp.dot(a_hi, w_lo, preferred_element_type=F32)
              + jnp.dot(a_lo, w_hi, preferred_element_type=F32))
    lane = lax.broadcasted_iota(jnp.int32, (tr, LANES), 1).astype(F32)
    neg = jnp.float32(-jnp.inf)
    logits = jnp.where(lane < N_EXPERTS, logits, neg)
    m1 = jnp.max(logits, axis=-1, keepdims=True)
    i1 = jnp.min(jnp.where(logits == m1, lane, float(LANES)), axis=-1, keepdims=True)
    rest = jnp.where(lane == i1, neg, logits)
    m2 = jnp.max(rest, axis=-1, keepdims=True)
    i2 = jnp.min(jnp.where(rest == m2, lane, float(LANES)), axis=-1, keepdims=True)
    e = jnp.exp(m2 - m1)
    g1 = 1.0 / (1.0 + e)
    g2 = e / (1.0 + e)
    pos = pl.program_id(1) * tr + lax.broadcasted_iota(jnp.int32, (tr, LANES), 0)
    valid = (pos >= PAD).astype(F32)
    oh1 = (lane == i1).astype(F32)
    oh2 = (lane == i2).astype(F32)
    chosen = (oh1 + oh2) * valid
    before = carry_scr[0:1, :] + jnp.dot(tri_ref[...], chosen.astype(BF16), preferred_element_type=F32)
    r1 = jnp.sum(before * oh1, axis=-1, keepdims=True)
    r2 = jnp.sum(before * oh2, axis=-1, keepdims=True)
    total = carry_scr[0:1, :] + jnp.sum(chosen, axis=0, keepdims=True)
    carry_scr[...] = jnp.broadcast_to(total, carry_scr.shape)
    cnt_ref[...] = jnp.broadcast_to(total, cnt_ref.shape)
    cols = (i1, i2, g1 * valid, g2 * valid, r1, r2)
    slab = jnp.zeros((tr, LANES), F32)
    for c, val in enumerate(cols):
        slab = jnp.where(lane == c, val, slab)
    slab_ref[...] = slab


def _router(h3, g, w_router):
    B, Lp, D = h3.shape
    tr = Lp // 2
    wr = jnp.zeros((D, LANES), F32).at[:, :N_EXPERTS].set(w_router)
    t = np.arange(tr)
    tri = (t[:, None] > t[None, :]).astype(np.float32)
    slab, cnt = pl.pallas_call(
        _router_kernel,
        grid=(B, Lp // tr),
        in_specs=[pl.BlockSpec((None, tr, D), lambda b, j: (b, j, 0)),
                  pl.BlockSpec((1, D), lambda b, j: (0, 0)),
                  pl.BlockSpec((D, LANES), lambda b, j: (0, 0)),
                  pl.BlockSpec((tr, tr), lambda b, j: (0, 0))],
        out_specs=[pl.BlockSpec((tr, LANES), lambda b, j: (b * (Lp // tr) + j, 0)),
                   pl.BlockSpec((8, LANES), lambda b, j: (0, 0))],
        out_shape=[jax.ShapeDtypeStruct((B * Lp, LANES), F32), jax.ShapeDtypeStruct((8, LANES), F32)],
        scratch_shapes=[pltpu.VMEM((8, LANES), F32)],
        compiler_params=_cparams("arbitrary", "arbitrary"),
        name="moe_router",
    )(h3, g.reshape(1, D), wr, jnp.asarray(tri, BF16))
    return slab, cnt[0, :N_EXPERTS]


def _invert_kernel(lo_ref, hi_ref, spare_ref, p1_hbm, p2_hbm, inv_ref, buf1, buf2, sem, *, n_tok):
    chunk = buf1.shape[0]
    tile_mask = spare_ref.shape[0] - 1

    def fill(r, carry):
        inv_ref[r] = spare_ref[r & tile_mask]
        return carry

    for e in range(N_EXPERTS + 1):
        lax.fori_loop(lo_ref[e], hi_ref[e], fill, 0)
    for c in range(n_tok // chunk):
        c1 = pltpu.make_async_copy(p1_hbm.at[pl.ds(c * chunk, chunk)], buf1, sem.at[0])
        c2 = pltpu.make_async_copy(p2_hbm.at[pl.ds(c * chunk, chunk)], buf2, sem.at[1])
        c1.start()
        c2.start()
        c1.wait()
        c2.wait()

        def token(i, carry, c=c):
            inv_ref[buf1[i]] = c * chunk + i
            inv_ref[buf2[i]] = n_tok + c * chunk + i
            return carry

        lax.fori_loop(0, chunk, token, 0, unroll=8)


def _invert(p1, p2, pad_lo, pad_hi, spare, n_rows):
    n_tok = p1.shape[0]
    chunk = _pick(n_tok, (2048, 1024, 512, 256, 128))
    grid_spec = pltpu.PrefetchScalarGridSpec(
        num_scalar_prefetch=3,
        grid=(1,),
        in_specs=[pl.BlockSpec(memory_space=pl.ANY), pl.BlockSpec(memory_space=pl.ANY)],
        out_specs=pl.BlockSpec(memory_space=pltpu.SMEM),
        scratch_shapes=[pltpu.SMEM((chunk,), jnp.int32), pltpu.SMEM((chunk,), jnp.int32),
                        pltpu.SemaphoreType.DMA((2,))])
    return pl.pallas_call(
        functools.partial(_invert_kernel, n_tok=n_tok),
        grid_spec=grid_spec,
        out_shape=jax.ShapeDtypeStruct((n_rows + 1,), jnp.int32),
        compiler_params=_cparams("arbitrary"),
        name="moe_invert",
    )(pad_lo, pad_hi, spare, p1, p2)


MOE_TILE = 256
MOE_GATES = 4


def _moe_ffn_kernel(te_ref, inv_ref, spare_ref, zero_ref, h_hbm, g_ref, w1_ref, w3_ref, w2_ref, y_hbm,
                    xbuf, ybuf, a_scr, gsem, ssem, *, n_tok, n_batch, seq_pad):
    del te_ref
    t = pl.program_id(0)
    n = pl.num_programs(0)
    tm = xbuf.shape[0]
    slot = t % 2
    other = 1 - slot

    def gather_row(tile, r, off=0):
        idx = inv_ref[tile * tm + r + off]
        src = jnp.where(idx >= n_tok, idx - n_tok, idx)
        pltpu.make_async_copy(h_hbm.at[pl.ds(src, 1)], xbuf.at[pl.ds(r, 1)], gsem.at[0]).start()

    def scatter_row(tile, buf_slot, r, off=0):
        dst = jnp.where(tile >= 0, inv_ref[jnp.maximum(tile, 0) * tm + r + off], spare_ref[r])
        pltpu.make_async_copy(ybuf.at[buf_slot, pl.ds(r, 1)], y_hbm.at[pl.ds(dst, 1)], ssem.at[buf_slot]).start()

    def wait_gather():
        pltpu.make_async_copy(h_hbm.at[pl.ds(0, tm)], xbuf, gsem.at[0]).wait()

    def wait_scatter(buf_slot):
        pltpu.make_async_copy(ybuf.at[buf_slot], y_hbm.at[pl.ds(0, tm)], ssem.at[buf_slot]).wait()

    @pl.when(t == 0)
    def _():
        ybuf[...] = jnp.zeros_like(ybuf)
        zero_copies = [pltpu.make_async_copy(ybuf.at[0, pl.ds(0, PAD)],
                                             y_hbm.at[pl.ds(k * n_tok + b * seq_pad, PAD)], ssem.at[0])
                       for k in range(2) for b in range(n_batch)]
        for cp in zero_copies:
            cp.start()
        for cp in zero_copies:
            cp.wait()

        def first(r, carry):
            gather_row(0, r)
            return carry

        lax.fori_loop(0, tm, first, 0, unroll=8)

    @pl.when(t > 0)
    def _():
        wait_scatter(slot)

    wait_gather()
    a_scr[...] = _rms(xbuf[...], g_ref[...]).astype(BF16)
    nxt = jnp.minimum(t + 1, n - 1)
    prv = t - 1
    n_chunks = w1_ref.shape[-1] // FFN_CHUNK
    bounds = [(c * tm) // n_chunks for c in range(n_chunks + 1)]
    for c in range(n_chunks):
        cs = slice(c * FFN_CHUNK, (c + 1) * FFN_CHUNK)
        a = a_scr[...]
        h1 = jnp.dot(a, w1_ref[0, :, cs], preferred_element_type=F32)
        h3 = jnp.dot(a, w3_ref[0, :, cs], preferred_element_type=F32)
        y = jnp.dot((_silu(h1) * h3).astype(BF16), w2_ref[0, cs, :], preferred_element_type=F32)
        if c == 0:
            ybuf[slot] = y
        else:
            ybuf[slot] += y
        gates = [res[rb * (tm // MOE_GATES):rb * (tm // MOE_GATES) + 1, 0:1]
                 for res in (h1, h3, y) for rb in range(MOE_GATES)]
        rows = list(range(bounds[c], bounds[c + 1]))
        for gi, gate in enumerate(gates):
            off = gate[0, 0].astype(jnp.int32) * zero_ref[0]
            for r in rows[(gi * len(rows)) // len(gates):((gi + 1) * len(rows)) // len(gates)]:
                gather_row(nxt, r, off)
                scatter_row(prv, other, r, off)

    @pl.when(t == n - 1)
    def _():
        def last(r, carry):
            scatter_row(t, slot, r)
            return carry

        wait_scatter(other)
        lax.fori_loop(0, tm, last, 0, unroll=8)
        wait_gather()
        wait_scatter(slot)


def _moe_ffn(h2, g, w1, w3, w2, tile_expert, inv, spare, n_rows, n_batch):
    R, D = h2.shape
    F = w1.shape[-1]
    tm = MOE_TILE
    kern = functools.partial(_moe_ffn_kernel, n_tok=R, n_batch=n_batch, seq_pad=R // n_batch)
    grid_spec = pltpu.PrefetchScalarGridSpec(
        num_scalar_prefetch=4,
        grid=(n_rows // tm,),
        in_specs=[pl.BlockSpec(memory_space=pl.ANY),
                  pl.BlockSpec((1, D), lambda t, te, inv, sp, z: (0, 0)),
                  pl.BlockSpec((1, D, F), lambda t, te, inv, sp, z: (te[t], 0, 0)),
                  pl.BlockSpec((1, D, F), lambda t, te, inv, sp, z: (te[t], 0, 0)),
                  pl.BlockSpec((1, F, D), lambda t, te, inv, sp, z: (te[t], 0, 0))],
        out_specs=pl.BlockSpec(memory_space=pl.ANY),
        scratch_shapes=[pltpu.VMEM((tm, D), F32), pltpu.VMEM((2, tm, D), F32), pltpu.VMEM((tm, D), BF16),
                        pltpu.SemaphoreType.DMA((1,)), pltpu.SemaphoreType.DMA((2,))])
    return pl.pallas_call(
        kern,
        grid_spec=grid_spec,
        out_shape=jax.ShapeDtypeStruct((2 * R, D), F32),
        compiler_params=_cparams("arbitrary"),
        name="swiglu_experts",
    )(tile_expert, inv, spare, jnp.zeros((1,), jnp.int32), h2, g.reshape(1, D), w1, w3, w2)


def _combine_kernel(h_ref, slab_ref, y1_ref, y2_ref, o_ref):
    slab = slab_ref[...]
    o_ref[...] = h_ref[...] + (slab[:, 2:3] * y1_ref[...] + slab[:, 3:4] * y2_ref[...])


def _combine(h2, slab, y):
    R, D = h2.shape
    tc = _pick(R, (512, 256, 128))
    return pl.pallas_call(
        _combine_kernel,
        grid=(R // tc,),
        in_specs=[pl.BlockSpec((tc, D), lambda i: (i, 0)),
                  pl.BlockSpec((tc, LANES), lambda i: (i, 0)),
                  pl.BlockSpec((tc, D), lambda i: (i, 0)),
                  pl.BlockSpec((tc, D), lambda i: (i + R // tc, 0))],
        out_specs=pl.BlockSpec((tc, D), lambda i: (i, 0)),
        out_shape=jax.ShapeDtypeStruct((R, D), F32),
        input_output_aliases={0: 0},
        compiler_params=_cparams("parallel"),
        name="moe_combine",
    )(h2, slab, y, y)


def _moe(h3, g, w_router, w1, w3, w2):
    B, Lp, D = h3.shape
    tm = MOE_TILE
    n_tokens = B * (Lp - PAD)
    n_tiles = (2 * n_tokens) // tm + N_EXPERTS
    n_rows = n_tiles * tm
    slab, counts = _router(h3, g, w_router)
    counts = counts.astype(jnp.int32)
    tiles_e = (counts + tm - 1) // tm
    end_tile = jnp.cumsum(tiles_e)
    offset = ((end_tile - tiles_e) * tm).astype(F32)
    n_used = end_tile[-1:]
    expert_ids = jnp.arange(N_EXPERTS, dtype=F32)
    e1, e2 = slab[:, 0:1], slab[:, 1:2]
    valid = (jnp.arange(B * Lp) % Lp) >= PAD
    off1 = jnp.sum(jnp.where(e1 == expert_ids[None, :], offset[None, :], 0.0), axis=-1)
    off2 = jnp.sum(jnp.where(e2 == expert_ids[None, :], offset[None, :], 0.0), axis=-1)
    p1 = jnp.where(valid, off1 + slab[:, 4], float(n_rows)).astype(jnp.int32)
    p2 = jnp.where(valid, off2 + slab[:, 5], float(n_rows)).astype(jnp.int32)
    tile_ids = jnp.arange(n_tiles, dtype=jnp.int32)
    last_tile = jnp.maximum(n_used - 1, 0)
    tile_expert = jnp.sum(jnp.minimum(tile_ids, last_tile)[:, None] >= end_tile[None, :], axis=-1).astype(jnp.int32)
    tile_expert = jnp.minimum(tile_expert, N_EXPERTS - 1)
    group_start = (end_tile - tiles_e) * tm
    pad_lo = jnp.concatenate([group_start + counts, n_used * tm]).astype(jnp.int32)
    pad_hi = jnp.concatenate([end_tile * tm, jnp.full((1,), n_rows + 1, jnp.int32)]).astype(jnp.int32)
    spare_rows = [k * B * Lp + b * Lp + o for k in range(2) for b in range(B) for o in range(PAD)]
    assert tm & (tm - 1) == 0 and len(spare_rows) >= tm
    spare = jnp.asarray(np.array(spare_rows[:tm], np.int32))
    inv = _invert(p1, p2, pad_lo, pad_hi, spare, n_rows)
    h2 = h3.reshape(B * Lp, D)
    ys = _moe_ffn(h2, g, w1, w3, w2, tile_expert, inv, spare, n_rows, B)
    return _combine(h2, slab, ys).reshape(B, Lp, D)


def _final_norm_kernel(h_ref, g_ref, o_ref):
    lead = h_ref.shape[0] - o_ref.shape[0]
    o_ref[...] = _rms(h_ref[lead:, :], g_ref[...])


def _final_norm(h3, g):
    B, Lp, D = h3.shape
    S = Lp - PAD - N_META
    return pl.pallas_call(
        _final_norm_kernel,
        grid=(B,),
        in_specs=[pl.BlockSpec((None, Lp, D), lambda b: (b, 0, 0)),
                  pl.BlockSpec((1, D), lambda b: (0, 0))],
        out_specs=pl.BlockSpec((None, S, D), lambda b: (b, 0, 0)),
        out_shape=jax.ShapeDtypeStruct((B, S, D), F32),
        compiler_params=_cparams("parallel"),
        name="final_norm",
    )(h3, g.reshape(1, D))


def kernel(x, meta_tokens, norm_mix_g, norm_ffn_g, ret_wq, ret_wk, ret_wv, ret_wg, ret_wo, ret_gn_g,
           sb_wqkv, sb_wo, pool_w, pool_scale, ffn_w1, ffn_w3, ffn_w2,
           moe_router, moe_w1, moe_w3, moe_w2, final_norm_g):
    B, S, D = x.shape
    Lp = PAD + N_META + S
    R = B * Lp
    depth = norm_mix_g.shape[0]
    meta = jnp.broadcast_to(meta_tokens.astype(x.dtype)[None], (B, N_META, D))
    h = jnp.concatenate([jnp.zeros((B, PAD, D), x.dtype), meta, x], axis=1)
    for i in range(depth):
        m, j = i % 3, i // 3
        if m == 0:
            w_in = jnp.concatenate([ret_wq[j], ret_wk[j], ret_wv[j], ret_wg[j]], axis=1).astype(BF16)
            p = _norm_matmul(h.reshape(R, D), norm_mix_g[i], w_in)
            y = _retention(p.reshape(B, Lp, -1), ret_gn_g[j])
            h = _matmul_residual(y.reshape(R, -1), ret_wo[j].astype(BF16), h.reshape(R, D)).reshape(B, Lp, D)
        elif m == 1:
            qkv = _norm_matmul(h.reshape(R, D), norm_mix_g[i], sb_wqkv[j].astype(BF16))
            y = _stick_breaking(qkv.reshape(B, Lp, -1))
            h = _matmul_residual(y.reshape(R, D), sb_wo[j].astype(BF16), h.reshape(R, D)).reshape(B, Lp, D)
        else:
            h = _pool_mixer(h, norm_mix_g[i], pool_w[j], pool_scale[j])
        c = i // 2
        if i % 2 == 0:
            h = _ffn(h.reshape(R, D), norm_ffn_g[i], ffn_w1[c].astype(BF16), ffn_w3[c].astype(BF16),
                     ffn_w2[c].astype(BF16)).reshape(B, Lp, D)
        else:
            h = _moe(h, norm_ffn_g[i], moe_router[c], moe_w1[c].astype(BF16), moe_w3[c].astype(BF16),
                     moe_w2[c].astype(BF16))
    return _final_norm(h, final_norm_g)
```

```python
import functools

import numpy as np
import jax
import jax.numpy as jnp
from jax import lax
from jax.experimental import pallas as pl
from jax.experimental.pallas import tpu as pltpu

F32 = jnp.float32
BF16 = jnp.bfloat16

D_MODEL = 1024
N_META = 16
BLOCK = 128
PAD = BLOCK - N_META
EPS = 1e-6
RET_HEADS = 8
RET_DK = D_MODEL // RET_HEADS
RET_DV = 2 * D_MODEL // RET_HEADS
ROPE_BASE = 10000.0
SB_HEADS = 16
SB_HD = D_MODEL // SB_HEADS
POOL_WINDOWS = (2, 4, 8, 16)
POOL_GD = D_MODEL // len(POOL_WINDOWS)
N_EXPERTS = 8
LANES = 128
VMEM_LIMIT_BYTES = 58 * 1024 * 1024


def _cparams(*sem):
    return pltpu.CompilerParams(dimension_semantics=sem, vmem_limit_bytes=VMEM_LIMIT_BYTES)


def _pick(n, cands):
    for c in cands:
        if n % c == 0:
            return c
    raise ValueError(f"no tile for {n} in {cands}")


def _rms(x, g):
    ms = jnp.mean(x * x, axis=-1, keepdims=True)
    return x * lax.rsqrt(ms + EPS) * g


def _silu(x):
    return x * (1.0 / (1.0 + jnp.exp(-x)))


def _norm_matmul_kernel(h_ref, g_ref, w_ref, o_ref, a_scr):
    @pl.when(pl.program_id(1) == 0)
    def _():
        a_scr[...] = _rms(h_ref[...], g_ref[...]).astype(BF16)

    o_ref[...] = jnp.dot(a_scr[...], w_ref[...], preferred_element_type=F32).astype(o_ref.dtype)


def _norm_matmul(h2, g, w):
    R, D = h2.shape
    N = w.shape[1]
    tm = _pick(R, (1024, 512, 256, 128))
    tn = _pick(N, (2048, 1536, 1024, 512, 256, 128))
    return pl.pallas_call(
        _norm_matmul_kernel,
        grid=(R // tm, N // tn),
        in_specs=[pl.BlockSpec((tm, D), lambda i, j: (i, 0)),
                  pl.BlockSpec((1, D), lambda i, j: (0, 0)),
                  pl.BlockSpec((D, tn), lambda i, j: (0, j))],
        out_specs=pl.BlockSpec((tm, tn), lambda i, j: (i, j)),
        out_shape=jax.ShapeDtypeStruct((R, N), BF16),
        scratch_shapes=[pltpu.VMEM((tm, D), BF16)],
        compiler_params=_cparams("parallel", "arbitrary"),
        name="norm_matmul",
    )(h2, g.reshape(1, D), w)


def _matmul_residual_kernel(y_ref, w_ref, h_ref, o_ref):
    o_ref[...] = h_ref[...] + jnp.dot(y_ref[...], w_ref[...], preferred_element_type=F32)


def _matmul_residual(y, w, h2):
    R, K = y.shape
    D = w.shape[1]
    tm = _pick(R, (512, 256, 128))
    return pl.pallas_call(
        _matmul_residual_kernel,
        grid=(R // tm,),
        in_specs=[pl.BlockSpec((tm, K), lambda i: (i, 0)),
                  pl.BlockSpec((K, D), lambda i: (0, 0)),
                  pl.BlockSpec((tm, D), lambda i: (i, 0))],
        out_specs=pl.BlockSpec((tm, D), lambda i: (i, 0)),
        out_shape=jax.ShapeDtypeStruct((R, D), F32),
        input_output_aliases={2: 0},
        compiler_params=_cparams("parallel"),
        name="matmul_residual",
    )(y, w, h2)


def _retention_kernel(p_ref, cos_ref, sin_ref, dec_ref, zeta_ref, xi_ref, cd_ref, gn_ref, o_ref, state_scr):
    @pl.when(pl.program_id(1) == 0)
    def _():
        state_scr[...] = jnp.zeros_like(state_scr)

    cos = cos_ref[...]
    sin = sin_ref[...]
    k0, v0, g0 = D_MODEL, 2 * D_MODEL, 4 * D_MODEL
    heads = range(RET_HEADS)
    vcol = [slice(v0 + h * RET_DV, v0 + (h + 1) * RET_DV) for h in heads]
    qb, kb, kz = [], [], []
    for h in heads:
        q = p_ref[:, h * RET_DK:(h + 1) * RET_DK].astype(F32)
        k = p_ref[:, k0 + h * RET_DK:k0 + (h + 1) * RET_DK].astype(F32)
        qr = q * cos + pltpu.roll(q, RET_DK // 2, 1) * sin
        kr = (k * cos + pltpu.roll(k, RET_DK // 2, 1) * sin) * (RET_DK ** -0.5)
        qb.append(qr.astype(BF16))
        kb.append(kr.astype(BF16))
        kz.append((kr * zeta_ref[h]).astype(BF16))
    s = [lax.dot_general(qb[h], kb[h], (((1,), (1,)), ((), ())), preferred_element_type=F32) for h in heads]
    cross = [jnp.dot(qb[h], state_scr[h].astype(BF16), preferred_element_type=F32) for h in heads]
    kv = [lax.dot_general(kz[h], p_ref[:, vcol[h]], (((0,), (0,)), ((), ())), preferred_element_type=F32)
          for h in heads]
    sb = [(s[h] * dec_ref[h]).astype(BF16) for h in heads]
    for h in heads:
        state_scr[h] = state_scr[h] * cd_ref[h] + kv[h]
    intra = [jnp.dot(sb[h], p_ref[:, vcol[h]], preferred_element_type=F32) for h in heads]
    for h in heads:
        y = intra[h] + cross[h] * xi_ref[h]
        mu = jnp.mean(y, axis=-1, keepdims=True)
        d = y - mu
        var = jnp.mean(d * d, axis=-1, keepdims=True)
        yn = d * lax.rsqrt(var + EPS) * gn_ref[:, h * RET_DV:(h + 1) * RET_DV]
        gate = p_ref[:, g0 + h * RET_DV:g0 + (h + 1) * RET_DV].astype(F32)
        o_ref[:, h * RET_DV:(h + 1) * RET_DV] = (_silu(gate) * yn).astype(o_ref.dtype)


def _retention_tables(Lp):
    half = RET_DK // 2
    inv = ROPE_BASE ** (-jnp.arange(half, dtype=F32) / half)
    pos = (jnp.arange(Lp) - PAD).astype(F32)
    ang = pos[:, None] * inv[None, :]
    cos, sin = jnp.cos(ang), jnp.sin(ang)
    cos2 = jnp.concatenate([cos, cos], axis=-1)
    sin2 = jnp.concatenate([-sin, sin], axis=-1)
    log_g = jnp.log1p(-jnp.exp2(-5.0 - jnp.arange(RET_HEADS, dtype=F32)))
    idx = jnp.arange(BLOCK, dtype=F32)
    diff = idx[:, None] - idx[None, :]
    decay = jnp.where(diff >= 0, jnp.exp(jnp.maximum(diff, 0.0)[None] * log_g[:, None, None]), 0.0)
    zeta = jnp.exp((BLOCK - 1 - idx)[None, :] * log_g[:, None])
    xi = jnp.exp((idx + 1)[None, :] * log_g[:, None])
    cd = jnp.exp(BLOCK * log_g)
    zeta_b = jnp.broadcast_to(zeta[:, :, None], (RET_HEADS, BLOCK, RET_DK))
    xi_b = jnp.broadcast_to(xi[:, :, None], (RET_HEADS, BLOCK, RET_DV))
    cd_b = jnp.broadcast_to(cd[:, None, None], (RET_HEADS, 1, RET_DV))
    return cos2, sin2, decay, zeta_b, xi_b, cd_b


def _retention(p3, gn_g):
    B, Lp, W = p3.shape
    cos2, sin2, decay, zeta_b, xi_b, cd_b = _retention_tables(Lp)
    const3 = lambda b, c: (0, 0, 0)
    return pl.pallas_call(
        _retention_kernel,
        grid=(B, Lp // BLOCK),
        in_specs=[pl.BlockSpec((None, BLOCK, W), lambda b, c: (b, c, 0)),
                  pl.BlockSpec((BLOCK, RET_DK), lambda b, c: (c, 0)),
                  pl.BlockSpec((BLOCK, RET_DK), lambda b, c: (c, 0)),
                  pl.BlockSpec((RET_HEADS, BLOCK, BLOCK), const3),
                  pl.BlockSpec((RET_HEADS, BLOCK, RET_DK), const3),
                  pl.BlockSpec((RET_HEADS, BLOCK, RET_DV), const3),
                  pl.BlockSpec((RET_HEADS, 1, RET_DV), const3),
                  pl.BlockSpec((1, 2 * D_MODEL), lambda b, c: (0, 0))],
        out_specs=pl.BlockSpec((None, BLOCK, 2 * D_MODEL), lambda b, c: (b, c, 0)),
        out_shape=jax.ShapeDtypeStruct((B, Lp, 2 * D_MODEL), BF16),
        scratch_shapes=[pltpu.VMEM((RET_HEADS, RET_DK, RET_DV), F32)],
        compiler_params=_cparams("parallel", "arbitrary"),
        name="retention",
    )(p3, cos2, sin2, decay, zeta_b, xi_b, cd_b, gn_g.reshape(1, 2 * D_MODEL))


SB_PAIRS = D_MODEL // LANES
LOG2E = 1.4426950408889634


def _sb_kernel(q_ref, k_ref, v_ref, u_ref, o_ref, q_scr, acc_scr, car_scr):
    i = pl.program_id(1)
    lane = lax.broadcasted_iota(jnp.int32, (BLOCK, LANES), 1)
    for p in range(SB_PAIRS):
        qs = q_ref[:, p * LANES:(p + 1) * LANES] * (SB_HD ** -0.5)
        zero = jnp.zeros_like(qs)
        q_scr[p, :BLOCK] = jnp.where(lane < SB_HD, qs, zero)
        q_scr[p, BLOCK:] = jnp.where(lane >= SB_HD, qs, zero)
    row = lax.broadcasted_iota(jnp.int32, (2 * BLOCK, BLOCK), 0)
    col = lax.broadcasted_iota(jnp.int32, (2 * BLOCK, BLOCK), 1)
    qpos = i * BLOCK + (row & (BLOCK - 1))
    acc_scr[...] = jnp.zeros_like(acc_scr)
    car_scr[...] = jnp.zeros_like(car_scr)

    def key_blocks(js, masked):
        rows = [pl.ds(pl.multiple_of(j * BLOCK, BLOCK), BLOCK) for j in js]
        if masked:
            masks = [((j * BLOCK + col) < qpos) & ((j * BLOCK + col) >= PAD) for j in js]
            sel = lambda b, x: jnp.where(masks[b], x, 0.0)
        else:
            sel = lambda b, x: x
        items = [(b, p) for b in range(len(js)) for p in range(SB_PAIRS)]
        cols = [slice(p * LANES, (p + 1) * LANES) for p in range(SB_PAIRS)]
        z = {(b, p): lax.dot_general(q_scr[p], k_ref[rows[b], cols[p]], (((1,), (1,)), ((), ())),
                                     preferred_element_type=F32) for b, p in items}
        log_beta = {k: jnp.minimum(z[k], 0.0) - jnp.log(1.0 + jnp.exp2(jnp.abs(z[k]) * (-LOG2E))) for k in items}
        log_1m = {(b, p): sel(b, log_beta[b, p] - z[b, p]).astype(BF16) for b, p in items}
        r = {k: jnp.dot(log_1m[k], u_ref[...], preferred_element_type=F32) for k in items}
        after = {p: car_scr[p] for p in range(SB_PAIRS)}
        w = {}
        for b, p in items:
            w[b, p] = sel(b, jnp.exp(log_beta[b, p] + (after[p] + r[b, p][:, :BLOCK]))).astype(BF16)
            after[p] = after[p] + r[b, p][:, BLOCK:]
        o = {(b, p): jnp.dot(w[b, p], v_ref[rows[b], cols[p]], preferred_element_type=F32) for b, p in items}
        for p in range(SB_PAIRS):
            acc_scr[p] += sum(o[b, p] for b in range(len(js)))
            car_scr[p] = after[p]

    key_blocks([i], True)

    n_int = jnp.maximum(i - 1, 0)

    def interior(t, carry):
        j = i - 1 - 2 * t
        key_blocks([j, j - 1], False)
        return carry

    lax.fori_loop(0, n_int // 2, interior, 0)

    @pl.when(n_int % 2 == 1)
    def _():
        key_blocks([1], False)

    @pl.when(i > 0)
    def _():
        key_blocks([0], True)
    for p in range(SB_PAIRS):
        o_ref[:, p * LANES:(p + 1) * LANES] = jnp.where(
            lane < SB_HD, acc_scr[p, :BLOCK], acc_scr[p, BLOCK:]).astype(o_ref.dtype)


def _stick_breaking(qkv3):
    B, Lp, _ = qkv3.shape
    j_idx = np.arange(BLOCK)
    u = np.concatenate([(j_idx[:, None] > j_idx[None, :]).astype(np.float32),
                        np.ones((BLOCK, BLOCK), np.float32)], axis=1)
    return pl.pallas_call(
        _sb_kernel,
        grid=(B, Lp // BLOCK),
        in_specs=[pl.BlockSpec((None, BLOCK, D_MODEL), lambda b, i: (b, i, 0)),
                  pl.BlockSpec((None, Lp, D_MODEL), lambda b, i: (b, 0, 1)),
                  pl.BlockSpec((None, Lp, D_MODEL), lambda b, i: (b, 0, 2)),
                  pl.BlockSpec((BLOCK, 2 * BLOCK), lambda b, i: (0, 0))],
        out_specs=pl.BlockSpec((None, BLOCK, D_MODEL), lambda b, i: (b, i, 0)),
        out_shape=jax.ShapeDtypeStruct((B, Lp, D_MODEL), BF16),
        scratch_shapes=[pltpu.VMEM((SB_PAIRS, 2 * BLOCK, LANES), BF16),
                        pltpu.VMEM((SB_PAIRS, 2 * BLOCK, LANES), F32),
                        pltpu.VMEM((SB_PAIRS, 2 * BLOCK, BLOCK), F32)],
        compiler_params=_cparams("parallel", "arbitrary"),
        name="stick_breaking",
    )(qkv3, qkv3, qkv3, jnp.asarray(u, BF16))


def _pool_kernel(h_ref, g_ref, bm_ref, bh_ref, w_ref, sc_ref, o_ref, prev_scr):
    prev_scr[...] = jnp.zeros_like(prev_scr)

    def tile(j, carry):
        rows = pl.ds(pl.multiple_of(j * BLOCK, BLOCK), BLOCK)
        x = h_ref[rows, :]
        pos = j * BLOCK + lax.broadcasted_iota(jnp.int32, (BLOCK, POOL_GD), 0) - PAD
        a = _rms(x, g_ref[...])
        a_hi = a.astype(BF16)
        a_lo = (a - a_hi.astype(F32)).astype(BF16)
        for g, win in enumerate(POOL_WINDOWS):
            sl = slice(g * POOL_GD, (g + 1) * POOL_GD)
            ws = (jnp.dot(bm_ref[g], a_hi[:, sl], preferred_element_type=F32)
                  + jnp.dot(bm_ref[g], a_lo[:, sl], preferred_element_type=F32)
                  + jnp.dot(bh_ref[g], prev_scr[0, :, sl], preferred_element_type=F32)
                  + jnp.dot(bh_ref[g], prev_scr[1, :, sl], preferred_element_type=F32))
            cnt = jnp.clip(pos + 1, 1, win).astype(F32)
            diff = ws / cnt - a[:, sl]
            y = jnp.dot(diff.astype(BF16), w_ref[g], preferred_element_type=F32) * sc_ref[:, sl]
            o_ref[rows, sl] = x[:, sl] + y
        prev_scr[0] = a_hi
        prev_scr[1] = a_lo
        return carry

    lax.fori_loop(0, h_ref.shape[0] // BLOCK, tile, 0, unroll=2)


def _pool_mixer(h3, g, w_pool, scale):
    B, Lp, D = h3.shape
    t = np.arange(BLOCK)
    d_main = t[:, None] - t[None, :]
    d_halo = t[:, None] + BLOCK - t[None, :]
    bm = np.stack([((d_main >= 0) & (d_main < w)) for w in POOL_WINDOWS]).astype(np.float32)
    bh = np.stack([(d_halo < w) for w in POOL_WINDOWS]).astype(np.float32)
    ng = len(POOL_WINDOWS)
    return pl.pallas_call(
        _pool_kernel,
        grid=(B,),
        in_specs=[pl.BlockSpec((None, Lp, D), lambda b: (b, 0, 0)),
                  pl.BlockSpec((1, D), lambda b: (0, 0)),
                  pl.BlockSpec((ng, BLOCK, BLOCK), lambda b: (0, 0, 0)),
                  pl.BlockSpec((ng, BLOCK, BLOCK), lambda b: (0, 0, 0)),
                  pl.BlockSpec((ng, POOL_GD, POOL_GD), lambda b: (0, 0, 0)),
                  pl.BlockSpec((1, D), lambda b: (0, 0))],
        out_specs=pl.BlockSpec((None, Lp, D), lambda b: (b, 0, 0)),
        out_shape=jax.ShapeDtypeStruct((B, Lp, D), F32),
        scratch_shapes=[pltpu.VMEM((2, BLOCK, D), BF16)],
        input_output_aliases={0: 0},
        compiler_params=_cparams("parallel"),
        name="pool_mixer",
    )(h3, g.reshape(1, D), jnp.asarray(bm, BF16), jnp.asarray(bh, BF16), w_pool.astype(BF16),
      scale.reshape(1, D))


FFN_CHUNK = 256


def _ffn_kernel(x_ref, g_ref, w1_ref, w3_ref, w2_ref, o_ref, a_scr):
    x = x_ref[...]
    a_scr[...] = _rms(x, g_ref[...]).astype(BF16)
    for c in range(w1_ref.shape[-1] // FFN_CHUNK):
        cs = slice(c * FFN_CHUNK, (c + 1) * FFN_CHUNK)
        a = a_scr[...]
        h1 = jnp.dot(a, w1_ref[:, cs], preferred_element_type=F32)
        h3 = jnp.dot(a, w3_ref[:, cs], preferred_element_type=F32)
        y = jnp.dot((_silu(h1) * h3).astype(BF16), w2_ref[cs, :], preferred_element_type=F32)
        if c == 0:
            o_ref[...] = x + y
        else:
            o_ref[...] += y


def _ffn(x2, g, w1, w3, w2):
    R, D = x2.shape
    F = w1.shape[-1]
    tm = _pick(R, (512, 256, 128))
    return pl.pallas_call(
        _ffn_kernel,
        grid=(R // tm,),
        in_specs=[pl.BlockSpec((tm, D), lambda t: (t, 0)),
                  pl.BlockSpec((1, D), lambda t: (0, 0)),
                  pl.BlockSpec((D, F), lambda t: (0, 0)),
                  pl.BlockSpec((D, F), lambda t: (0, 0)),
                  pl.BlockSpec((F, D), lambda t: (0, 0))],
        out_specs=pl.BlockSpec((tm, D), lambda t: (t, 0)),
        out_shape=jax.ShapeDtypeStruct((R, D), F32),
        scratch_shapes=[pltpu.VMEM((tm, D), BF16)],
        input_output_aliases={0: 0},
        compiler_params=_cparams("parallel"),
        name="swiglu_dense",
    )(x2, g.reshape(1, D), w1, w3, w2)


def _router_kernel(h_ref, g_ref, wr_ref, tri_ref, slab_ref, cnt_ref, carry_scr):
    first = (pl.program_id(0) == 0) & (pl.program_id(1) == 0)

    @pl.when(first)
    def _():
        carry_scr[...] = jnp.zeros_like(carry_scr)

    tr = h_ref.shape[0]
    a = _rms(h_ref[...], g_ref[...])
    a_hi = a.astype(BF16)
    a_lo = (a - a_hi.astype(F32)).astype(BF16)
    wr = wr_ref[...]
    w_hi = wr.astype(BF16)
    w_lo = (wr - w_hi.astype(F32)).astype(BF16)
    logits = (jnp.dot(a_hi, w_hi, preferred_element_type=F32)
              + jnp.dot(a_hi, w_lo, preferred_element_type=F32)
              + jnp.dot(a_lo, w_hi, preferred_element_type=F32))
    lane = lax.broadcasted_iota(jnp.int32, (tr, LANES), 1).astype(F32)
    neg = jnp.float32(-jnp.inf)
    logits = jnp.where(lane < N_EXPERTS, logits, neg)
    m1 = jnp.max(logits, axis=-1, keepdims=True)
    i1 = jnp.min(jnp.where(logits == m1, lane, float(LANES)), axis=-1, keepdims=True)
    rest = jnp.where(lane == i1, neg, logits)
    m2 = jnp.max(rest, axis=-1, keepdims=True)
    i2 = jnp.min(jnp.where(rest == m2, lane, float(LANES)), axis=-1, keepdims=True)
    e = jnp.exp(m2 - m1)
    g1 = 1.0 / (1.0 + e)
    g2 = e / (1.0 + e)
    pos = pl.program_id(1) * tr + lax.broadcasted_iota(jnp.int32, (tr, LANES), 0)
    valid = (pos >= PAD).astype(F32)
    oh1 = (lane == i1).astype(F32)
    oh2 = (lane == i2).astype(F32)
    chosen = (oh1 + oh2) * valid
    before = carry_scr[0:1, :] + jnp.dot(tri_ref[...], chosen.astype(BF16), preferred_element_type=F32)
    r1 = jnp.sum(before * oh1, axis=-1, keepdims=True)
    r2 = jnp.sum(before * oh2, axis=-1, keepdims=True)
    total = carry_scr[0:1, :] + jnp.sum(chosen, axis=0, keepdims=True)
    carry_scr[...] = jnp.broadcast_to(total, carry_scr.shape)
    cnt_ref[...] = jnp.broadcast_to(total, cnt_ref.shape)
    cols = (i1, i2, g1 * valid, g2 * valid, r1, r2)
    slab = jnp.zeros((tr, LANES), F32)
    for c, val in enumerate(cols):
        slab = jnp.where(lane == c, val, slab)
    slab_ref[...] = slab


def _router(h3, g, w_router):
    B, Lp, D = h3.shape
    tr = Lp // 2
    wr = jnp.zeros((D, LANES), F32).at[:, :N_EXPERTS].set(w_router)
    t = np.arange(tr)
    tri = (t[:, None] > t[None, :]).astype(np.float32)
    slab, cnt = pl.pallas_call(
        _router_kernel,
        grid=(B, Lp // tr),
        in_specs=[pl.BlockSpec((None, tr, D), lambda b, j: (b, j, 0)),
                  pl.BlockSpec((1, D), lambda b, j: (0, 0)),
                  pl.BlockSpec((D, LANES), lambda b, j: (0, 0)),
                  pl.BlockSpec((tr, tr), lambda b, j: (0, 0))],
        out_specs=[pl.BlockSpec((tr, LANES), lambda b, j: (b * (Lp // tr) + j, 0)),
                   pl.BlockSpec((8, LANES), lambda b, j: (0, 0))],
        out_shape=[jax.ShapeDtypeStruct((B * Lp, LANES), F32), jax.ShapeDtypeStruct((8, LANES), F32)],
        scratch_shapes=[pltpu.VMEM((8, LANES), F32)],
        compiler_params=_cparams("arbitrary", "arbitrary"),
        name="moe_router",
    )(h3, g.reshape(1, D), wr, jnp.asarray(tri, BF16))
    return slab, cnt[0, :N_EXPERTS]


def _invert_kernel(lo_ref, hi_ref, spare_ref, p1_hbm, p2_hbm, inv_ref, buf1, buf2, sem, *, n_tok):
    chunk = buf1.shape[0]
    tile_mask = spare_ref.shape[0] - 1

    def fill(r, carry):
        inv_ref[r] = spare_ref[r & tile_mask]
        return carry

    for e in range(N_EXPERTS + 1):
        lax.fori_loop(lo_ref[e], hi_ref[e], fill, 0)
    for c in range(n_tok // chunk):
        c1 = pltpu.make_async_copy(p1_hbm.at[pl.ds(c * chunk, chunk)], buf1, sem.at[0])
        c2 = pltpu.make_async_copy(p2_hbm.at[pl.ds(c * chunk, chunk)], buf2, sem.at[1])
        c1.start()
        c2.start()
        c1.wait()
        c2.wait()

        def token(i, carry, c=c):
            inv_ref[buf1[i]] = c * chunk + i
            inv_ref[buf2[i]] = n_tok + c * chunk + i
            return carry

        lax.fori_loop(0, chunk, token, 0, unroll=8)


def _invert(p1, p2, pad_lo, pad_hi, spare, n_rows):
    n_tok = p1.shape[0]
    chunk = _pick(n_tok, (2048, 1024, 512, 256, 128))
    grid_spec = pltpu.PrefetchScalarGridSpec(
        num_scalar_prefetch=3,
        grid=(1,),
        in_specs=[pl.BlockSpec(memory_space=pl.ANY), pl.BlockSpec(memory_space=pl.ANY)],
        out_specs=pl.BlockSpec(memory_space=pltpu.SMEM),
        scratch_shapes=[pltpu.SMEM((chunk,), jnp.int32), pltpu.SMEM((chunk,), jnp.int32),
                        pltpu.SemaphoreType.DMA((2,))])
    return pl.pallas_call(
        functools.partial(_invert_kernel, n_tok=n_tok),
        grid_spec=grid_spec,
        out_shape=jax.ShapeDtypeStruct((n_rows + 1,), jnp.int32),
        compiler_params=_cparams("arbitrary"),
        name="moe_invert",
    )(pad_lo, pad_hi, spare, p1, p2)


MOE_TILE = 512
MOE_GATES = 4
MOE_GATHER_SLACK = 2


def _moe_ffn_kernel(te_ref, inv_ref, spare_ref, zero_ref, h_hbm, g_ref, w1_ref, w3_ref, w2_ref, y_hbm,
                    xbuf, ybuf, a_scr, gsem, ssem, *, n_tok, n_batch, seq_pad):
    del te_ref
    t = pl.program_id(0)
    n = pl.num_programs(0)
    tm = xbuf.shape[0]
    slot = t % 2
    other = 1 - slot

    def gather_row(tile, r, off=0):
        idx = inv_ref[tile * tm + r + off]
        src = jnp.where(idx >= n_tok, idx - n_tok, idx)
        pltpu.make_async_copy(h_hbm.at[pl.ds(src, 1)], xbuf.at[pl.ds(r, 1)], gsem.at[0]).start()

    def scatter_row(tile, buf_slot, r, off=0):
        dst = jnp.where(tile >= 0, inv_ref[jnp.maximum(tile, 0) * tm + r + off], spare_ref[r])
        pltpu.make_async_copy(ybuf.at[buf_slot, pl.ds(r, 1)], y_hbm.at[pl.ds(dst, 1)], ssem.at[buf_slot]).start()

    def wait_gather():
        pltpu.make_async_copy(h_hbm.at[pl.ds(0, tm)], xbuf, gsem.at[0]).wait()

    def wait_scatter(buf_slot):
        pltpu.make_async_copy(ybuf.at[buf_slot], y_hbm.at[pl.ds(0, tm)], ssem.at[buf_slot]).wait()

    @pl.when(t == 0)
    def _():
        ybuf[...] = jnp.zeros_like(ybuf)
        zero_copies = [pltpu.make_async_copy(ybuf.at[0, pl.ds(0, PAD)],
                                             y_hbm.at[pl.ds(k * n_tok + b * seq_pad, PAD)], ssem.at[0])
                       for k in range(2) for b in range(n_batch)]
        for cp in zero_copies:
            cp.start()
        for cp in zero_copies:
            cp.wait()

        def first(r, carry):
            gather_row(0, r)
            return carry

        lax.fori_loop(0, tm, first, 0, unroll=8)

    @pl.when(t > 0)
    def _():
        wait_scatter(slot)

    wait_gather()
    a_scr[...] = _rms(xbuf[...], g_ref[...]).astype(BF16)
    nxt = jnp.minimum(t + 1, n - 1)
    prv = t - 1
    n_chunks = w1_ref.shape[-1] // FFN_CHUNK
    bounds = [(c * tm) // n_chunks for c in range(n_chunks + 1)]
    g_chunks = n_chunks - MOE_GATHER_SLACK
    g_bounds = [(min(c, g_chunks) * tm) // g_chunks for c in range(n_chunks + 1)]
    for c in range(n_chunks):
        cs = slice(c * FFN_CHUNK, (c + 1) * FFN_CHUNK)
        a = a_scr[...]
        h1 = jnp.dot(a, w1_ref[0, :, cs], preferred_element_type=F32)
        h3 = jnp.dot(a, w3_ref[0, :, cs], preferred_element_type=F32)
        y = jnp.dot((_silu(h1) * h3).astype(BF16), w2_ref[0, cs, :], preferred_element_type=F32)
        if c == 0:
            ybuf[slot] = y
        else:
            ybuf[slot] += y
        gates = [res[rb * (tm // MOE_GATES):rb * (tm // MOE_GATES) + 1, 0:1]
                 for res in (h1, h3, y) for rb in range(MOE_GATES)]
        s_rows = list(range(bounds[c], bounds[c + 1]))
        g_rows = list(range(g_bounds[c], g_bounds[c + 1]))
        for gi, gate in enumerate(gates):
            off = gate[0, 0].astype(jnp.int32) * zero_ref[0]
            for r in g_rows[(gi * len(g_rows)) // len(gates):((gi + 1) * len(g_rows)) // len(gates)]:
                gather_row(nxt, r, off)
            for r in s_rows[(gi * len(s_rows)) // len(gates):((gi + 1) * len(s_rows)) // len(gates)]:
                scatter_row(prv, other, r, off)

    @pl.when(t == n - 1)
    def _():
        def last(r, carry):
            scatter_row(t, slot, r)
            return carry

        wait_scatter(other)
        lax.fori_loop(0, tm, last, 0, unroll=8)
        wait_gather()
        wait_scatter(slot)


def _moe_ffn(h2, g, w1, w3, w2, tile_expert, inv, spare, n_rows, n_batch):
    R, D = h2.shape
    F = w1.shape[-1]
    tm = MOE_TILE
    kern = functools.partial(_moe_ffn_kernel, n_tok=R, n_batch=n_batch, seq_pad=R // n_batch)
    grid_spec = pltpu.PrefetchScalarGridSpec(
        num_scalar_prefetch=4,
        grid=(n_rows // tm,),
        in_specs=[pl.BlockSpec(memory_space=pl.ANY),
                  pl.BlockSpec((1, D), lambda t, te, inv, sp, z: (0, 0)),
                  pl.BlockSpec((1, D, F), lambda t, te, inv, sp, z: (te[t], 0, 0)),
                  pl.BlockSpec((1, D, F), lambda t, te, inv, sp, z: (te[t], 0, 0)),
                  pl.BlockSpec((1, F, D), lambda t, te, inv, sp, z: (te[t], 0, 0))],
        out_specs=pl.BlockSpec(memory_space=pl.ANY),
        scratch_shapes=[pltpu.VMEM((tm, D), F32), pltpu.VMEM((2, tm, D), F32), pltpu.VMEM((tm, D), BF16),
                        pltpu.SemaphoreType.DMA((1,)), pltpu.SemaphoreType.DMA((2,))])
    return pl.pallas_call(
        kern,
        grid_spec=grid_spec,
        out_shape=jax.ShapeDtypeStruct((2 * R, D), F32),
        compiler_params=_cparams("arbitrary"),
        name="swiglu_experts",
    )(tile_expert, inv, spare, jnp.zeros((1,), jnp.int32), h2, g.reshape(1, D), w1, w3, w2)


def _combine_kernel(h_ref, slab_ref, y1_ref, y2_ref, o_ref):
    slab = slab_ref[...]
    o_ref[...] = h_ref[...] + (slab[:, 2:3] * y1_ref[...] + slab[:, 3:4] * y2_ref[...])


def _combine(h2, slab, y):
    R, D = h2.shape
    tc = _pick(R, (512, 256, 128))
    return pl.pallas_call(
        _combine_kernel,
        grid=(R // tc,),
        in_specs=[pl.BlockSpec((tc, D), lambda i: (i, 0)),
                  pl.BlockSpec((tc, LANES), lambda i: (i, 0)),
                  pl.BlockSpec((tc, D), lambda i: (i, 0)),
                  pl.BlockSpec((tc, D), lambda i: (i + R // tc, 0))],
        out_specs=pl.BlockSpec((tc, D), lambda i: (i, 0)),
        out_shape=jax.ShapeDtypeStruct((R, D), F32),
        input_output_aliases={0: 0},
        compiler_params=_cparams("parallel"),
        name="moe_combine",
    )(h2, slab, y, y)


def _moe(h3, g, w_router, w1, w3, w2):
    B, Lp, D = h3.shape
    tm = MOE_TILE
    n_tokens = B * (Lp - PAD)
    n_tiles = (2 * n_tokens) // tm + N_EXPERTS
    n_rows = n_tiles * tm
    slab, counts = _router(h3, g, w_router)
    counts = counts.astype(jnp.int32)
    tiles_e = (counts + tm - 1) // tm
    end_tile = jnp.cumsum(tiles_e)
    offset = ((end_tile - tiles_e) * tm).astype(F32)
    n_used = end_tile[-1:]
    expert_ids = jnp.arange(N_EXPERTS, dtype=F32)
    e1, e2 = slab[:, 0:1], slab[:, 1:2]
    valid = (jnp.arange(B * Lp) % Lp) >= PAD
    off1 = jnp.sum(jnp.where(e1 == expert_ids[None, :], offset[None, :], 0.0), axis=-1)
    off2 = jnp.sum(jnp.where(e2 == expert_ids[None, :], offset[None, :], 0.0), axis=-1)
    p1 = jnp.where(valid, off1 + slab[:, 4], float(n_rows)).astype(jnp.int32)
    p2 = jnp.where(valid, off2 + slab[:, 5], float(n_rows)).astype(jnp.int32)
    tile_ids = jnp.arange(n_tiles, dtype=jnp.int32)
    last_tile = jnp.maximum(n_used - 1, 0)
    tile_expert = jnp.sum(jnp.minimum(tile_ids, last_tile)[:, None] >= end_tile[None, :], axis=-1).astype(jnp.int32)
    tile_expert = jnp.minimum(tile_expert, N_EXPERTS - 1)
    group_start = (end_tile - tiles_e) * tm
    pad_lo = jnp.concatenate([group_start + counts, n_used * tm]).astype(jnp.int32)
    pad_hi = jnp.concatenate([end_tile * tm, jnp.full((1,), n_rows + 1, jnp.int32)]).astype(jnp.int32)
    spare_rows = [k * B * Lp + b * Lp + o for k in range(2) for b in range(B) for o in range(PAD)]
    assert tm & (tm - 1) == 0 and len(spare_rows) >= tm
    spare = jnp.asarray(np.array(spare_rows[:tm], np.int32))
    inv = _invert(p1, p2, pad_lo, pad_hi, spare, n_rows)
    h2 = h3.reshape(B * Lp, D)
    ys = _moe_ffn(h2, g, w1, w3, w2, tile_expert, inv, spare, n_rows, B)
    return _combine(h2, slab, ys).reshape(B, Lp, D)


def _final_norm_kernel(h_ref, g_ref, o_ref):
    lead = h_ref.shape[0] - o_ref.shape[0]
    o_ref[...] = _rms(h_ref[lead:, :], g_ref[...])


def _final_norm(h3, g):
    B, Lp, D = h3.shape
    S = Lp - PAD - N_META
    return pl.pallas_call(
        _final_norm_kernel,
        grid=(B,),
        in_specs=[pl.BlockSpec((None, Lp, D), lambda b: (b, 0, 0)),
                  pl.BlockSpec((1, D), lambda b: (0, 0))],
        out_specs=pl.BlockSpec((None, S, D), lambda b: (b, 0, 0)),
        out_shape=jax.ShapeDtypeStruct((B, S, D), F32),
        compiler_params=_cparams("parallel"),
        name="final_norm",
    )(h3, g.reshape(1, D))


def kernel(x, meta_tokens, norm_mix_g, norm_ffn_g, ret_wq, ret_wk, ret_wv, ret_wg, ret_wo, ret_gn_g,
           sb_wqkv, sb_wo, pool_w, pool_scale, ffn_w1, ffn_w3, ffn_w2,
           moe_router, moe_w1, moe_w3, moe_w2, final_norm_g):
    B, S, D = x.shape
    Lp = PAD + N_META + S
    R = B * Lp
    depth = norm_mix_g.shape[0]
    meta = jnp.broadcast_to(meta_tokens.astype(x.dtype)[None], (B, N_META, D))
    h = jnp.concatenate([jnp.zeros((B, PAD, D), x.dtype), meta, x], axis=1)
    for i in range(depth):
        m, j = i % 3, i // 3
        if m == 0:
            w_in = jnp.concatenate([ret_wq[j], ret_wk[j], ret_wv[j], ret_wg[j]], axis=1).astype(BF16)
            p = _norm_matmul(h.reshape(R, D), norm_mix_g[i], w_in)
            y = _retention(p.reshape(B, Lp, -1), ret_gn_g[j])
            h = _matmul_residual(y.reshape(R, -1), ret_wo[j].astype(BF16), h.reshape(R, D)).reshape(B, Lp, D)
        elif m == 1:
            qkv = _norm_matmul(h.reshape(R, D), norm_mix_g[i], sb_wqkv[j].astype(BF16))
            y = _stick_breaking(qkv.reshape(B, Lp, -1))
            h = _matmul_residual(y.reshape(R, D), sb_wo[j].astype(BF16), h.reshape(R, D)).reshape(B, Lp, D)
        else:
            h = _pool_mixer(h, norm_mix_g[i], pool_w[j], pool_scale[j])
        c = i // 2
        if i % 2 == 0:
            h = _ffn(h.reshape(R, D), norm_ffn_g[i], ffn_w1[c].astype(BF16), ffn_w3[c].astype(BF16),
                     ffn_w2[c].astype(BF16)).reshape(B, Lp, D)
        else:
            h = _moe(h, norm_ffn_g[i], moe_router[c], moe_w1[c].astype(BF16), moe_w3[c].astype(BF16),
                     moe_w2[c].astype(BF16))
    return _final_norm(h, final_norm_g)
```

```python
import functools

import numpy as np
import jax
import jax.numpy as jnp
from jax import lax
from jax.experimental import pallas as pl
from jax.experimental.pallas import tpu as pltpu

F32 = jnp.float32
BF16 = jnp.bfloat16

D_MODEL = 1024
N_META = 16
BLOCK = 128
PAD = BLOCK - N_META
EPS = 1e-6
RET_HEADS = 8
RET_DK = D_MODEL // RET_HEADS
RET_DV = 2 * D_MODEL // RET_HEADS
ROPE_BASE = 10000.0
SB_HEADS = 16
SB_HD = D_MODEL // SB_HEADS
POOL_WINDOWS = (2, 4, 8, 16)
POOL_GD = D_MODEL // len(POOL_WINDOWS)
N_EXPERTS = 8
LANES = 128
VMEM_LIMIT_BYTES = 58 * 1024 * 1024


def _cparams(*sem):
    return pltpu.CompilerParams(dimension_semantics=sem, vmem_limit_bytes=VMEM_LIMIT_BYTES)


def _pick(n, cands):
    for c in cands:
        if n % c == 0:
            return c
    raise ValueError(f"no tile for {n} in {cands}")


def _rms(x, g):
    ms = jnp.mean(x * x, axis=-1, keepdims=True)
    return x * lax.rsqrt(ms + EPS) * g


def _silu(x):
    return x * (1.0 / (1.0 + jnp.exp(-x)))


def _norm_matmul_kernel(h_ref, g_ref, w_ref, o_ref, a_scr):
    @pl.when(pl.program_id(1) == 0)
    def _():
        a_scr[...] = _rms(h_ref[...], g_ref[...]).astype(BF16)

    o_ref[...] = jnp.dot(a_scr[...], w_ref[...], preferred_element_type=F32).astype(o_ref.dtype)


def _norm_matmul(h2, g, w):
    R, D = h2.shape
    N = w.shape[1]
    tm = _pick(R, (1024, 512, 256, 128))
    tn = _pick(N, (2048, 1536, 1024, 512, 256, 128))
    return pl.pallas_call(
        _norm_matmul_kernel,
        grid=(R // tm, N // tn),
        in_specs=[pl.BlockSpec((tm, D), lambda i, j: (i, 0)),
                  pl.BlockSpec((1, D), lambda i, j: (0, 0)),
                  pl.BlockSpec((D, tn), lambda i, j: (0, j))],
        out_specs=pl.BlockSpec((tm, tn), lambda i, j: (i, j)),
        out_shape=jax.ShapeDtypeStruct((R, N), BF16),
        scratch_shapes=[pltpu.VMEM((tm, D), BF16)],
        compiler_params=_cparams("parallel", "arbitrary"),
        name="norm_matmul",
    )(h2, g.reshape(1, D), w)


def _matmul_residual_kernel(y_ref, w_ref, h_ref, o_ref):
    o_ref[...] = h_ref[...] + jnp.dot(y_ref[...], w_ref[...], preferred_element_type=F32)


def _matmul_residual(y, w, h2):
    R, K = y.shape
    D = w.shape[1]
    tm = _pick(R, (512, 256, 128))
    return pl.pallas_call(
        _matmul_residual_kernel,
        grid=(R // tm,),
        in_specs=[pl.BlockSpec((tm, K), lambda i: (i, 0)),
                  pl.BlockSpec((K, D), lambda i: (0, 0)),
                  pl.BlockSpec((tm, D), lambda i: (i, 0))],
        out_specs=pl.BlockSpec((tm, D), lambda i: (i, 0)),
        out_shape=jax.ShapeDtypeStruct((R, D), F32),
        input_output_aliases={2: 0},
        compiler_params=_cparams("parallel"),
        name="matmul_residual",
    )(y, w, h2)


def _retention_kernel(p_ref, cos_ref, sin_ref, dec_ref, zeta_ref, xi_ref, cd_ref, gn_ref, o_ref, state_scr):
    @pl.when(pl.program_id(1) == 0)
    def _():
        state_scr[...] = jnp.zeros_like(state_scr)

    cos = cos_ref[...]
    sin = sin_ref[...]
    k0, v0, g0 = D_MODEL, 2 * D_MODEL, 4 * D_MODEL
    heads = range(RET_HEADS)
    vcol = [slice(v0 + h * RET_DV, v0 + (h + 1) * RET_DV) for h in heads]
    qb, kb, kz = [], [], []
    for h in heads:
        q = p_ref[:, h * RET_DK:(h + 1) * RET_DK].astype(F32)
        k = p_ref[:, k0 + h * RET_DK:k0 + (h + 1) * RET_DK].astype(F32)
        qr = q * cos + pltpu.roll(q, RET_DK // 2, 1) * sin
        kr = (k * cos + pltpu.roll(k, RET_DK // 2, 1) * sin) * (RET_DK ** -0.5)
        qb.append(qr.astype(BF16))
        kb.append(kr.astype(BF16))
        kz.append((kr * zeta_ref[h]).astype(BF16))
    s = [lax.dot_general(qb[h], kb[h], (((1,), (1,)), ((), ())), preferred_element_type=F32) for h in heads]
    cross = [jnp.dot(qb[h], state_scr[h].astype(BF16), preferred_element_type=F32) for h in heads]
    kv = [lax.dot_general(kz[h], p_ref[:, vcol[h]], (((0,), (0,)), ((), ())), preferred_element_type=F32)
          for h in heads]
    sb = [(s[h] * dec_ref[h]).astype(BF16) for h in heads]
    for h in heads:
        state_scr[h] = state_scr[h] * cd_ref[h] + kv[h]
    intra = [jnp.dot(sb[h], p_ref[:, vcol[h]], preferred_element_type=F32) for h in heads]
    for h in heads:
        y = intra[h] + cross[h] * xi_ref[h]
        mu = jnp.mean(y, axis=-1, keepdims=True)
        d = y - mu
        var = jnp.mean(d * d, axis=-1, keepdims=True)
        yn = d * lax.rsqrt(var + EPS) * gn_ref[:, h * RET_DV:(h + 1) * RET_DV]
        gate = p_ref[:, g0 + h * RET_DV:g0 + (h + 1) * RET_DV].astype(F32)
        o_ref[:, h * RET_DV:(h + 1) * RET_DV] = (_silu(gate) * yn).astype(o_ref.dtype)


def _retention_tables(Lp):
    half = RET_DK // 2
    inv = ROPE_BASE ** (-jnp.arange(half, dtype=F32) / half)
    pos = (jnp.arange(Lp) - PAD).astype(F32)
    ang = pos[:, None] * inv[None, :]
    cos, sin = jnp.cos(ang), jnp.sin(ang)
    cos2 = jnp.concatenate([cos, cos], axis=-1)
    sin2 = jnp.concatenate([-sin, sin], axis=-1)
    log_g = jnp.log1p(-jnp.exp2(-5.0 - jnp.arange(RET_HEADS, dtype=F32)))
    idx = jnp.arange(BLOCK, dtype=F32)
    diff = idx[:, None] - idx[None, :]
    decay = jnp.where(diff >= 0, jnp.exp(jnp.maximum(diff, 0.0)[None] * log_g[:, None, None]), 0.0)
    zeta = jnp.exp((BLOCK - 1 - idx)[None, :] * log_g[:, None])
    xi = jnp.exp((idx + 1)[None, :] * log_g[:, None])
    cd = jnp.exp(BLOCK * log_g)
    zeta_b = jnp.broadcast_to(zeta[:, :, None], (RET_HEADS, BLOCK, RET_DK))
    xi_b = jnp.broadcast_to(xi[:, :, None], (RET_HEADS, BLOCK, RET_DV))
    cd_b = jnp.broadcast_to(cd[:, None, None], (RET_HEADS, 1, RET_DV))
    return cos2, sin2, decay, zeta_b, xi_b, cd_b


def _retention(p3, gn_g):
    B, Lp, W = p3.shape
    cos2, sin2, decay, zeta_b, xi_b, cd_b = _retention_tables(Lp)
    const3 = lambda b, c: (0, 0, 0)
    return pl.pallas_call(
        _retention_kernel,
        grid=(B, Lp // BLOCK),
        in_specs=[pl.BlockSpec((None, BLOCK, W), lambda b, c: (b, c, 0)),
                  pl.BlockSpec((BLOCK, RET_DK), lambda b, c: (c, 0)),
                  pl.BlockSpec((BLOCK, RET_DK), lambda b, c: (c, 0)),
                  pl.BlockSpec((RET_HEADS, BLOCK, BLOCK), const3),
                  pl.BlockSpec((RET_HEADS, BLOCK, RET_DK), const3),
                  pl.BlockSpec((RET_HEADS, BLOCK, RET_DV), const3),
                  pl.BlockSpec((RET_HEADS, 1, RET_DV), const3),
                  pl.BlockSpec((1, 2 * D_MODEL), lambda b, c: (0, 0))],
        out_specs=pl.BlockSpec((None, BLOCK, 2 * D_MODEL), lambda b, c: (b, c, 0)),
        out_shape=jax.ShapeDtypeStruct((B, Lp, 2 * D_MODEL), BF16),
        scratch_shapes=[pltpu.VMEM((RET_HEADS, RET_DK, RET_DV), F32)],
        compiler_params=_cparams("parallel", "arbitrary"),
        name="retention",
    )(p3, cos2, sin2, decay, zeta_b, xi_b, cd_b, gn_g.reshape(1, 2 * D_MODEL))


SB_PAIRS = D_MODEL // LANES
LOG2E = 1.4426950408889634


def _sb_kernel(q_ref, k_ref, v_ref, u_ref, o_ref, q_scr, acc_scr, car_scr):
    i = pl.program_id(1)
    lane = lax.broadcasted_iota(jnp.int32, (BLOCK, LANES), 1)
    for p in range(SB_PAIRS):
        qs = q_ref[:, p * LANES:(p + 1) * LANES] * (SB_HD ** -0.5)
        zero = jnp.zeros_like(qs)
        q_scr[p, :BLOCK] = jnp.where(lane < SB_HD, qs, zero)
        q_scr[p, BLOCK:] = jnp.where(lane >= SB_HD, qs, zero)
    row = lax.broadcasted_iota(jnp.int32, (2 * BLOCK, BLOCK), 0)
    col = lax.broadcasted_iota(jnp.int32, (2 * BLOCK, BLOCK), 1)
    qpos = i * BLOCK + (row & (BLOCK - 1))
    acc_scr[...] = jnp.zeros_like(acc_scr)
    car_scr[...] = jnp.zeros_like(car_scr)

    def key_blocks(js, masked):
        rows = [pl.ds(pl.multiple_of(j * BLOCK, BLOCK), BLOCK) for j in js]
        if masked:
            masks = [((j * BLOCK + col) < qpos) & ((j * BLOCK + col) >= PAD) for j in js]
            sel = lambda b, x: jnp.where(masks[b], x, 0.0)
        else:
            sel = lambda b, x: x
        items = [(b, p) for b in range(len(js)) for p in range(SB_PAIRS)]
        cols = [slice(p * LANES, (p + 1) * LANES) for p in range(SB_PAIRS)]
        z = {(b, p): lax.dot_general(q_scr[p], k_ref[rows[b], cols[p]], (((1,), (1,)), ((), ())),
                                     preferred_element_type=F32) for b, p in items}
        log_beta = {k: jnp.minimum(z[k], 0.0) - jnp.log(1.0 + jnp.exp2(jnp.abs(z[k]) * (-LOG2E))) for k in items}
        log_1m = {(b, p): sel(b, log_beta[b, p] - z[b, p]).astype(BF16) for b, p in items}
        r = {k: jnp.dot(log_1m[k], u_ref[...], preferred_element_type=F32) for k in items}
        after = {p: car_scr[p] for p in range(SB_PAIRS)}
        w = {}
        for b, p in items:
            w[b, p] = sel(b, jnp.exp(log_beta[b, p] + (after[p] + r[b, p][:, :BLOCK]))).astype(BF16)
            after[p] = after[p] + r[b, p][:, BLOCK:]
        o = {(b, p): jnp.dot(w[b, p], v_ref[rows[b], cols[p]], preferred_element_type=F32) for b, p in items}
        for p in range(SB_PAIRS):
            acc_scr[p] += sum(o[b, p] for b in range(len(js)))
            car_scr[p] = after[p]

    key_blocks([i], True)

    n_int = jnp.maximum(i - 1, 0)

    def interior(t, carry):
        j = i - 1 - 2 * t
        key_blocks([j, j - 1], False)
        return carry

    lax.fori_loop(0, n_int // 2, interior, 0)

    @pl.when(n_int % 2 == 1)
    def _():
        key_blocks([1], False)

    @pl.when(i > 0)
    def _():
        key_blocks([0], True)
    for p in range(SB_PAIRS):
        o_ref[:, p * LANES:(p + 1) * LANES] = jnp.where(
            lane < SB_HD, acc_scr[p, :BLOCK], acc_scr[p, BLOCK:]).astype(o_ref.dtype)


def _stick_breaking(qkv3):
    B, Lp, _ = qkv3.shape
    j_idx = np.arange(BLOCK)
    u = np.concatenate([(j_idx[:, None] > j_idx[None, :]).astype(np.float32),
                        np.ones((BLOCK, BLOCK), np.float32)], axis=1)
    return pl.pallas_call(
        _sb_kernel,
        grid=(B, Lp // BLOCK),
        in_specs=[pl.BlockSpec((None, BLOCK, D_MODEL), lambda b, i: (b, i, 0)),
                  pl.BlockSpec((None, Lp, D_MODEL), lambda b, i: (b, 0, 1)),
                  pl.BlockSpec((None, Lp, D_MODEL), lambda b, i: (b, 0, 2)),
                  pl.BlockSpec((BLOCK, 2 * BLOCK), lambda b, i: (0, 0))],
        out_specs=pl.BlockSpec((None, BLOCK, D_MODEL), lambda b, i: (b, i, 0)),
        out_shape=jax.ShapeDtypeStruct((B, Lp, D_MODEL), BF16),
        scratch_shapes=[pltpu.VMEM((SB_PAIRS, 2 * BLOCK, LANES), BF16),
                        pltpu.VMEM((SB_PAIRS, 2 * BLOCK, LANES), F32),
                        pltpu.VMEM((SB_PAIRS, 2 * BLOCK, BLOCK), F32)],
        compiler_params=_cparams("parallel", "arbitrary"),
        name="stick_breaking",
    )(qkv3, qkv3, qkv3, jnp.asarray(u, BF16))


def _pool_kernel(h_ref, g_ref, bm_ref, bh_ref, w_ref, sc_ref, o_ref, prev_scr):
    prev_scr[...] = jnp.zeros_like(prev_scr)

    def tile(j, carry):
        rows = pl.ds(pl.multiple_of(j * BLOCK, BLOCK), BLOCK)
        x = h_ref[rows, :]
        pos = j * BLOCK + lax.broadcasted_iota(jnp.int32, (BLOCK, POOL_GD), 0) - PAD
        a = _rms(x, g_ref[...])
        a_hi = a.astype(BF16)
        a_lo = (a - a_hi.astype(F32)).astype(BF16)
        for g, win in enumerate(POOL_WINDOWS):
            sl = slice(g * POOL_GD, (g + 1) * POOL_GD)
            ws = (jnp.dot(bm_ref[g], a_hi[:, sl], preferred_element_type=F32)
                  + jnp.dot(bm_ref[g], a_lo[:, sl], preferred_element_type=F32)
                  + jnp.dot(bh_ref[g], prev_scr[0, :, sl], preferred_element_type=F32)
                  + jnp.dot(bh_ref[g], prev_scr[1, :, sl], preferred_element_type=F32))
            cnt = jnp.clip(pos + 1, 1, win).astype(F32)
            diff = ws / cnt - a[:, sl]
            y = jnp.dot(diff.astype(BF16), w_ref[g], preferred_element_type=F32) * sc_ref[:, sl]
            o_ref[rows, sl] = x[:, sl] + y
        prev_scr[0] = a_hi
        prev_scr[1] = a_lo
        return carry

    lax.fori_loop(0, h_ref.shape[0] // BLOCK, tile, 0, unroll=2)


def _pool_mixer(h3, g, w_pool, scale):
    B, Lp, D = h3.shape
    t = np.arange(BLOCK)
    d_main = t[:, None] - t[None, :]
    d_halo = t[:, None] + BLOCK - t[None, :]
    bm = np.stack([((d_main >= 0) & (d_main < w)) for w in POOL_WINDOWS]).astype(np.float32)
    bh = np.stack([(d_halo < w) for w in POOL_WINDOWS]).astype(np.float32)
    ng = len(POOL_WINDOWS)
    return pl.pallas_call(
        _pool_kernel,
        grid=(B,),
        in_specs=[pl.BlockSpec((None, Lp, D), lambda b: (b, 0, 0)),
                  pl.BlockSpec((1, D), lambda b: (0, 0)),
                  pl.BlockSpec((ng, BLOCK, BLOCK), lambda b: (0, 0, 0)),
                  pl.BlockSpec((ng, BLOCK, BLOCK), lambda b: (0, 0, 0)),
                  pl.BlockSpec((ng, POOL_GD, POOL_GD), lambda b: (0, 0, 0)),
                  pl.BlockSpec((1, D), lambda b: (0, 0))],
        out_specs=pl.BlockSpec((None, Lp, D), lambda b: (b, 0, 0)),
        out_shape=jax.ShapeDtypeStruct((B, Lp, D), F32),
        scratch_shapes=[pltpu.VMEM((2, BLOCK, D), BF16)],
        input_output_aliases={0: 0},
        compiler_params=_cparams("parallel"),
        name="pool_mixer",
    )(h3, g.reshape(1, D), jnp.asarray(bm, BF16), jnp.asarray(bh, BF16), w_pool.astype(BF16),
      scale.reshape(1, D))


FFN_CHUNK = 256


def _ffn_kernel(x_ref, g_ref, w1_ref, w3_ref, w2_ref, o_ref, a_scr):
    x = x_ref[...]
    a_scr[...] = _rms(x, g_ref[...]).astype(BF16)
    for c in range(w1_ref.shape[-1] // FFN_CHUNK):
        cs = slice(c * FFN_CHUNK, (c + 1) * FFN_CHUNK)
        a = a_scr[...]
        h1 = jnp.dot(a, w1_ref[:, cs], preferred_element_type=F32)
        h3 = jnp.dot(a, w3_ref[:, cs], preferred_element_type=F32)
        y = jnp.dot((_silu(h1) * h3).astype(BF16), w2_ref[cs, :], preferred_element_type=F32)
        if c == 0:
            o_ref[...] = x + y
        else:
            o_ref[...] += y


def _ffn(x2, g, w1, w3, w2):
    R, D = x2.shape
    F = w1.shape[-1]
    tm = _pick(R, (512, 256, 128))
    return pl.pallas_call(
        _ffn_kernel,
        grid=(R // tm,),
        in_specs=[pl.BlockSpec((tm, D), lambda t: (t, 0)),
                  pl.BlockSpec((1, D), lambda t: (0, 0)),
                  pl.BlockSpec((D, F), lambda t: (0, 0)),
                  pl.BlockSpec((D, F), lambda t: (0, 0)),
                  pl.BlockSpec((F, D), lambda t: (0, 0))],
        out_specs=pl.BlockSpec((tm, D), lambda t: (t, 0)),
        out_shape=jax.ShapeDtypeStruct((R, D), F32),
        scratch_shapes=[pltpu.VMEM((tm, D), BF16)],
        input_output_aliases={0: 0},
        compiler_params=_cparams("parallel"),
        name="swiglu_dense",
    )(x2, g.reshape(1, D), w1, w3, w2)


def _router_kernel(h_ref, g_ref, wr_ref, tri_ref, slab_ref, cnt_ref, carry_scr):
    first = (pl.program_id(0) == 0) & (pl.program_id(1) == 0)

    @pl.when(first)
    def _():
        carry_scr[...] = jnp.zeros_like(carry_scr)

    tr = h_ref.shape[0]
    a = _rms(h_ref[...], g_ref[...])
    a_hi = a.astype(BF16)
    a_lo = (a - a_hi.astype(F32)).astype(BF16)
    wr = wr_ref[...]
    w_hi = wr.astype(BF16)
    w_lo = (wr - w_hi.astype(F32)).astype(BF16)
    logits = (jnp.dot(a_hi, w_hi, preferred_element_type=F32)
              + jnp.dot(a_hi, w_lo, preferred_element_type=F32)
              + jnp.dot(a_lo, w_hi, preferred_element_type=F32))
    lane = lax.broadcasted_iota(jnp.int32, (tr, LANES), 1).astype(F32)
    neg = jnp.float32(-jnp.inf)
    logits = jnp.where(lane < N_EXPERTS, logits, neg)
    m1 = jnp.max(logits, axis=-1, keepdims=True)
    i1 = jnp.min(jnp.where(logits == m1, lane, float(LANES)), axis=-1, keepdims=True)
    rest = jnp.where(lane == i1, neg, logits)
    m2 = jnp.max(rest, axis=-1, keepdims=True)
    i2 = jnp.min(jnp.where(rest == m2, lane, float(LANES)), axis=-1, keepdims=True)
    e = jnp.exp(m2 - m1)
    g1 = 1.0 / (1.0 + e)
    g2 = e / (1.0 + e)
    pos = pl.program_id(1) * tr + lax.broadcasted_iota(jnp.int32, (tr, LANES), 0)
    valid = (pos >= PAD).astype(F32)
    oh1 = (lane == i1).astype(F32)
    oh2 = (lane == i2).astype(F32)
    chosen = (oh1 + oh2) * valid
    before = carry_scr[0:1, :] + jnp.dot(tri_ref[...], chosen.astype(BF16), preferred_element_type=F32)
    r1 = jnp.sum(before * oh1, axis=-1, keepdims=True)
    r2 = jnp.sum(before * oh2, axis=-1, keepdims=True)
    total = carry_scr[0:1, :] + jnp.sum(chosen, axis=0, keepdims=True)
    carry_scr[...] = jnp.broadcast_to(total, carry_scr.shape)
    cnt_ref[...] = jnp.broadcast_to(total, cnt_ref.shape)
    cols = (i1, i2, g1 * valid, g2 * valid, r1, r2)
    slab = jnp.zeros((tr, LANES), F32)
    for c, val in enumerate(cols):
        slab = jnp.where(lane == c, val, slab)
    slab_ref[...] = slab


def _router(h3, g, w_router):
    B, Lp, D = h3.shape
    tr = Lp // 2
    wr = jnp.zeros((D, LANES), F32).at[:, :N_EXPERTS].set(w_router)
    t = np.arange(tr)
    tri = (t[:, None] > t[None, :]).astype(np.float32)
    slab, cnt = pl.pallas_call(
        _router_kernel,
        grid=(B, Lp // tr),
        in_specs=[pl.BlockSpec((None, tr, D), lambda b, j: (b, j, 0)),
                  pl.BlockSpec((1, D), lambda b, j: (0, 0)),
                  pl.BlockSpec((D, LANES), lambda b, j: (0, 0)),
                  pl.BlockSpec((tr, tr), lambda b, j: (0, 0))],
        out_specs=[pl.BlockSpec((tr, LANES), lambda b, j: (b * (Lp // tr) + j, 0)),
                   pl.BlockSpec((8, LANES), lambda b, j: (0, 0))],
        out_shape=[jax.ShapeDtypeStruct((B * Lp, LANES), F32), jax.ShapeDtypeStruct((8, LANES), F32)],
        scratch_shapes=[pltpu.VMEM((8, LANES), F32)],
        compiler_params=_cparams("arbitrary", "arbitrary"),
        name="moe_router",
    )(h3, g.reshape(1, D), wr, jnp.asarray(tri, BF16))
    return slab, cnt[0, :N_EXPERTS]


def _invert_kernel(lo_ref, hi_ref, spare_ref, p1_hbm, p2_hbm, inv_ref, buf1, buf2, sem, *, n_tok):
    chunk = buf1.shape[0]
    tile_mask = spare_ref.shape[0] - 1

    def fill(r, carry):
        inv_ref[r] = spare_ref[r & tile_mask]
        return carry

    for e in range(N_EXPERTS + 1):
        lax.fori_loop(lo_ref[e], hi_ref[e], fill, 0)
    for c in range(n_tok // chunk):
        c1 = pltpu.make_async_copy(p1_hbm.at[pl.ds(c * chunk, chunk)], buf1, sem.at[0])
        c2 = pltpu.make_async_copy(p2_hbm.at[pl.ds(c * chunk, chunk)], buf2, sem.at[1])
        c1.start()
        c2.start()
        c1.wait()
        c2.wait()

        def token(i, carry, c=c):
            inv_ref[buf1[i]] = c * chunk + i
            inv_ref[buf2[i]] = n_tok + c * chunk + i
            return carry

        lax.fori_loop(0, chunk, token, 0, unroll=8)


def _invert(p1, p2, pad_lo, pad_hi, spare, n_rows):
    n_tok = p1.shape[0]
    chunk = _pick(n_tok, (2048, 1024, 512, 256, 128))
    grid_spec = pltpu.PrefetchScalarGridSpec(
        num_scalar_prefetch=3,
        grid=(1,),
        in_specs=[pl.BlockSpec(memory_space=pl.ANY), pl.BlockSpec(memory_space=pl.ANY)],
        out_specs=pl.BlockSpec(memory_space=pltpu.SMEM),
        scratch_shapes=[pltpu.SMEM((chunk,), jnp.int32), pltpu.SMEM((chunk,), jnp.int32),
                        pltpu.SemaphoreType.DMA((2,))])
    return pl.pallas_call(
        functools.partial(_invert_kernel, n_tok=n_tok),
        grid_spec=grid_spec,
        out_shape=jax.ShapeDtypeStruct((n_rows + 1,), jnp.int32),
        compiler_params=_cparams("arbitrary"),
        name="moe_invert",
    )(pad_lo, pad_hi, spare, p1, p2)


MOE_TILE = 512
MOE_GATES = 4
MOE_GATHER_SLACK = 2


def _moe_ffn_kernel(te_ref, inv_ref, spare_ref, zero_ref, h_hbm, g_ref, w1_ref, w3_ref, w2_ref, y_hbm,
                    xbuf, ybuf, a_scr, gsem, ssem, *, n_tok, n_batch, seq_pad):
    del te_ref
    t = pl.program_id(0)
    n = pl.num_programs(0)
    tm = xbuf.shape[0]
    slot = t % 2
    other = 1 - slot

    def gather_row(tile, r, off=0, prio=0):
        idx = inv_ref[tile * tm + r + off]
        src = jnp.where(idx >= n_tok, idx - n_tok, idx)
        pltpu.make_async_copy(h_hbm.at[pl.ds(src, 1)], xbuf.at[pl.ds(r, 1)], gsem.at[0]).start(priority=prio)

    def scatter_row(tile, buf_slot, r, off=0, prio=0):
        dst = jnp.where(tile >= 0, inv_ref[jnp.maximum(tile, 0) * tm + r + off], spare_ref[r])
        pltpu.make_async_copy(ybuf.at[buf_slot, pl.ds(r, 1)], y_hbm.at[pl.ds(dst, 1)],
                              ssem.at[buf_slot]).start(priority=prio)

    def wait_gather():
        pltpu.make_async_copy(h_hbm.at[pl.ds(0, tm)], xbuf, gsem.at[0]).wait()

    def wait_scatter(buf_slot):
        pltpu.make_async_copy(ybuf.at[buf_slot], y_hbm.at[pl.ds(0, tm)], ssem.at[buf_slot]).wait()

    @pl.when(t == 0)
    def _():
        ybuf[...] = jnp.zeros_like(ybuf)
        zero_copies = [pltpu.make_async_copy(ybuf.at[0, pl.ds(0, PAD)],
                                             y_hbm.at[pl.ds(k * n_tok + b * seq_pad, PAD)], ssem.at[0])
                       for k in range(2) for b in range(n_batch)]
        for cp in zero_copies:
            cp.start()
        for cp in zero_copies:
            cp.wait()

        def first(r, carry):
            gather_row(0, r)
            return carry

        lax.fori_loop(0, tm, first, 0, unroll=8)

    @pl.when(t > 0)
    def _():
        wait_scatter(slot)

    wait_gather()
    a_scr[...] = _rms(xbuf[...], g_ref[...]).astype(BF16)
    nxt = jnp.minimum(t + 1, n - 1)
    prv = t - 1
    n_chunks = w1_ref.shape[-1] // FFN_CHUNK
    bounds = [(c * tm) // n_chunks for c in range(n_chunks + 1)]
    g_chunks = n_chunks - MOE_GATHER_SLACK
    g_bounds = [(min(c, g_chunks) * tm) // g_chunks for c in range(n_chunks + 1)]
    for c in range(n_chunks):
        cs = slice(c * FFN_CHUNK, (c + 1) * FFN_CHUNK)
        a = a_scr[...]
        h1 = jnp.dot(a, w1_ref[0, :, cs], preferred_element_type=F32)
        h3 = jnp.dot(a, w3_ref[0, :, cs], preferred_element_type=F32)
        y = jnp.dot((_silu(h1) * h3).astype(BF16), w2_ref[0, cs, :], preferred_element_type=F32)
        if c == 0:
            ybuf[slot] = y
        else:
            ybuf[slot] += y
        gates = [res[rb * (tm // MOE_GATES):rb * (tm // MOE_GATES) + 1, 0:1]
                 for res in (h1, h3, y) for rb in range(MOE_GATES)]
        s_rows = list(range(bounds[c], bounds[c + 1]))
        g_rows = list(range(g_bounds[c], g_bounds[c + 1]))
        for gi, gate in enumerate(gates):
            off = gate[0, 0].astype(jnp.int32) * zero_ref[0]
            for r in g_rows[(gi * len(g_rows)) // len(gates):((gi + 1) * len(g_rows)) // len(gates)]:
                gather_row(nxt, r, off, prio=r % 2)
            for r in s_rows[(gi * len(s_rows)) // len(gates):((gi + 1) * len(s_rows)) // len(gates)]:
                scatter_row(prv, other, r, off, prio=r % 2)

    @pl.when(t == n - 1)
    def _():
        def last(r, carry):
            scatter_row(t, slot, r)
            return carry

        wait_scatter(other)
        lax.fori_loop(0, tm, last, 0, unroll=8)
        wait_gather()
        wait_scatter(slot)


def _moe_ffn(h2, g, w1, w3, w2, tile_expert, inv, spare, n_rows, n_batch):
    R, D = h2.shape
    F = w1.shape[-1]
    tm = MOE_TILE
    kern = functools.partial(_moe_ffn_kernel, n_tok=R, n_batch=n_batch, seq_pad=R // n_batch)
    grid_spec = pltpu.PrefetchScalarGridSpec(
        num_scalar_prefetch=4,
        grid=(n_rows // tm,),
        in_specs=[pl.BlockSpec(memory_space=pl.ANY),
                  pl.BlockSpec((1, D), lambda t, te, inv, sp, z: (0, 0)),
                  pl.BlockSpec((1, D, F), lambda t, te, inv, sp, z: (te[t], 0, 0)),
                  pl.BlockSpec((1, D, F), lambda t, te, inv, sp, z: (te[t], 0, 0)),
                  pl.BlockSpec((1, F, D), lambda t, te, inv, sp, z: (te[t], 0, 0))],
        out_specs=pl.BlockSpec(memory_space=pl.ANY),
        scratch_shapes=[pltpu.VMEM((tm, D), F32), pltpu.VMEM((2, tm, D), F32), pltpu.VMEM((tm, D), BF16),
                        pltpu.SemaphoreType.DMA((1,)), pltpu.SemaphoreType.DMA((2,))])
    return pl.pallas_call(
        kern,
        grid_spec=grid_spec,
        out_shape=jax.ShapeDtypeStruct((2 * R, D), F32),
        compiler_params=_cparams("arbitrary"),
        name="swiglu_experts",
    )(tile_expert, inv, spare, jnp.zeros((1,), jnp.int32), h2, g.reshape(1, D), w1, w3, w2)


def _combine_kernel(h_ref, slab_ref, y1_ref, y2_ref, o_ref):
    slab = slab_ref[...]
    o_ref[...] = h_ref[...] + (slab[:, 2:3] * y1_ref[...] + slab[:, 3:4] * y2_ref[...])


def _combine(h2, slab, y):
    R, D = h2.shape
    tc = _pick(R, (512, 256, 128))
    return pl.pallas_call(
        _combine_kernel,
        grid=(R // tc,),
        in_specs=[pl.BlockSpec((tc, D), lambda i: (i, 0)),
                  pl.BlockSpec((tc, LANES), lambda i: (i, 0)),
                  pl.BlockSpec((tc, D), lambda i: (i, 0)),
                  pl.BlockSpec((tc, D), lambda i: (i + R // tc, 0))],
        out_specs=pl.BlockSpec((tc, D), lambda i: (i, 0)),
        out_shape=jax.ShapeDtypeStruct((R, D), F32),
        input_output_aliases={0: 0},
        compiler_params=_cparams("parallel"),
        name="moe_combine",
    )(h2, slab, y, y)


def _moe(h3, g, w_router, w1, w3, w2):
    B, Lp, D = h3.shape
    tm = MOE_TILE
    n_tokens = B * (Lp - PAD)
    n_tiles = (2 * n_tokens) // tm + N_EXPERTS
    n_rows = n_tiles * tm
    slab, counts = _router(h3, g, w_router)
    counts = counts.astype(jnp.int32)
    tiles_e = (counts + tm - 1) // tm
    end_tile = jnp.cumsum(tiles_e)
    offset = ((end_tile - tiles_e) * tm).astype(F32)
    n_used = end_tile[-1:]
    expert_ids = jnp.arange(N_EXPERTS, dtype=F32)
    e1, e2 = slab[:, 0:1], slab[:, 1:2]
    valid = (jnp.arange(B * Lp) % Lp) >= PAD
    off1 = jnp.sum(jnp.where(e1 == expert_ids[None, :], offset[None, :], 0.0), axis=-1)
    off2 = jnp.sum(jnp.where(e2 == expert_ids[None, :], offset[None, :], 0.0), axis=-1)
    p1 = jnp.where(valid, off1 + slab[:, 4], float(n_rows)).astype(jnp.int32)
    p2 = jnp.where(valid, off2 + slab[:, 5], float(n_rows)).astype(jnp.int32)
    tile_ids = jnp.arange(n_tiles, dtype=jnp.int32)
    last_tile = jnp.maximum(n_used - 1, 0)
    tile_expert = jnp.sum(jnp.minimum(tile_ids, last_tile)[:, None] >= end_tile[None, :], axis=-1).astype(jnp.int32)
    tile_expert = jnp.minimum(tile_expert, N_EXPERTS - 1)
    group_start = (end_tile - tiles_e) * tm
    pad_lo = jnp.concatenate([group_start + counts, n_used * tm]).astype(jnp.int32)
    pad_hi = jnp.concatenate([end_tile * tm, jnp.full((1,), n_rows + 1, jnp.int32)]).astype(jnp.int32)
    spare_rows = [k * B * Lp + b * Lp + o for k in range(2) for b in range(B) for o in range(PAD)]
    assert tm & (tm - 1) == 0 and len(spare_rows) >= tm
    spare = jnp.asarray(np.array(spare_rows[:tm], np.int32))
    inv = _invert(p1, p2, pad_lo, pad_hi, spare, n_rows)
    h2 = h3.reshape(B * Lp, D)
    ys = _moe_ffn(h2, g, w1, w3, w2, tile_expert, inv, spare, n_rows, B)
    return _combine(h2, slab, ys).reshape(B, Lp, D)


def _final_norm_kernel(h_ref, g_ref, o_ref):
    lead = h_ref.shape[0] - o_ref.shape[0]
    o_ref[...] = _rms(h_ref[lead:, :], g_ref[...])


def _final_norm(h3, g):
    B, Lp, D = h3.shape
    S = Lp - PAD - N_META
    return pl.pallas_call(
        _final_norm_kernel,
        grid=(B,),
        in_specs=[pl.BlockSpec((None, Lp, D), lambda b: (b, 0, 0)),
                  pl.BlockSpec((1, D), lambda b: (0, 0))],
        out_specs=pl.BlockSpec((None, S, D), lambda b: (b, 0, 0)),
        out_shape=jax.ShapeDtypeStruct((B, S, D), F32),
        compiler_params=_cparams("parallel"),
        name="final_norm",
    )(h3, g.reshape(1, D))


def kernel(x, meta_tokens, norm_mix_g, norm_ffn_g, ret_wq, ret_wk, ret_wv, ret_wg, ret_wo, ret_gn_g,
           sb_wqkv, sb_wo, pool_w, pool_scale, ffn_w1, ffn_w3, ffn_w2,
           moe_router, moe_w1, moe_w3, moe_w2, final_norm_g):
    B, S, D = x.shape
    Lp = PAD + N_META + S
    R = B * Lp
    depth = norm_mix_g.shape[0]
    meta = jnp.broadcast_to(meta_tokens.astype(x.dtype)[None], (B, N_META, D))
    h = jnp.concatenate([jnp.zeros((B, PAD, D), x.dtype), meta, x], axis=1)
    for i in range(depth):
        m, j = i % 3, i // 3
        if m == 0:
            w_in = jnp.concatenate([ret_wq[j], ret_wk[j], ret_wv[j], ret_wg[j]], axis=1).astype(BF16)
            p = _norm_matmul(h.reshape(R, D), norm_mix_g[i], w_in)
            y = _retention(p.reshape(B, Lp, -1), ret_gn_g[j])
            h = _matmul_residual(y.reshape(R, -1), ret_wo[j].astype(BF16), h.reshape(R, D)).reshape(B, Lp, D)
        elif m == 1:
            qkv = _norm_matmul(h.reshape(R, D), norm_mix_g[i], sb_wqkv[j].astype(BF16))
            y = _stick_breaking(qkv.reshape(B, Lp, -1))
            h = _matmul_residual(y.reshape(R, D), sb_wo[j].astype(BF16), h.reshape(R, D)).reshape(B, Lp, D)
        else:
            h = _pool_mixer(h, norm_mix_g[i], pool_w[j], pool_scale[j])
        c = i // 2
        if i % 2 == 0:
            h = _ffn(h.reshape(R, D), norm_ffn_g[i], ffn_w1[c].astype(BF16), ffn_w3[c].astype(BF16),
                     ffn_w2[c].astype(BF16)).reshape(B, Lp, D)
        else:
            h = _moe(h, norm_ffn_g[i], moe_router[c], moe_w1[c].astype(BF16), moe_w3[c].astype(BF16),
                     moe_w2[c].astype(BF16))
    return _final_norm(h, final_norm_g)
```
